```python
import jax
import jax.numpy as jnp
from jax import lax
import numpy as np

D_MODEL = 1024
BATCH = 4
SEQ = 4096
DEPTH = 4
DEC_BATCH = 32
DEC_SEQ = 4
PAST_LEN = 8192
PAGE_SIZE = 128

HEAD_DIM = 64
N_HEADS = D_MODEL // HEAD_DIM
A_HEADS = N_HEADS // 2
A_KV = 2
IDX_HEADS = 4
IDX_DIM = 64
DSA_TOPK = 256
B_HEADS = N_HEADS - A_HEADS
B_KV = 2
CMP_LEN = 32
CMP_STRIDE = 16
SEL_LEN = 64
SEL_TOP = 16
WINDOW = 512
C_HEADS = N_HEADS
D_FF = 4 * D_MODEL
ROPE_THETA = 10000.0
RMS_EPS = 1e-6
Q_BLOCK = 128
NEG_BIG = -1e30
FORCE_SCORE = 1e9
N_EVEN = (DEPTH + 1) // 2
N_ODD = DEPTH // 2
EVEN_SIZES = (A_HEADS * HEAD_DIM, A_KV * HEAD_DIM, A_KV * HEAD_DIM,
              IDX_HEADS * IDX_DIM, IDX_DIM, IDX_HEADS,
              B_HEADS * HEAD_DIM,
              B_KV * HEAD_DIM, B_KV * HEAD_DIM,
              B_KV * HEAD_DIM, B_KV * HEAD_DIM,
              B_KV * HEAD_DIM, B_KV * HEAD_DIM,
              3 * B_HEADS)
EVEN_IN = sum(EVEN_SIZES)
ODD_SIZES = (C_HEADS * HEAD_DIM, C_HEADS * HEAD_DIM, C_HEADS * HEAD_DIM, C_HEADS)
ODD_IN = sum(ODD_SIZES)
F32 = jnp.float32

kernel_name = 'hybrid_dsa_nsa_fox_decode_step'


def rms_norm(x, g):
    xf = x.astype(F32)
    y = xf * lax.rsqrt(jnp.mean(xf * xf, axis=-1, keepdims=True) + RMS_EPS)
    return (y * g.astype(F32)).astype(x.dtype)


def rope(x, pos):
    half = x.shape[-1] // 2
    inv = ROPE_THETA ** (-jnp.arange(half, dtype=F32) / half)
    ang = pos.astype(F32)[:, None] * inv[None, :]
    cos, sin = jnp.cos(ang)[:, None, :], jnp.sin(ang)[:, None, :]
    xf = x.astype(F32)
    x1, x2 = xf[..., :half], xf[..., half:]
    return jnp.concatenate([x1 * cos - x2 * sin, x2 * cos + x1 * sin], axis=-1).astype(x.dtype)


def masked_softmax(s, mask):
    s = jnp.where(mask, s, NEG_BIG)
    p = jnp.exp(s - jnp.max(s, axis=-1, keepdims=True)) * mask
    return p / jnp.maximum(jnp.sum(p, axis=-1, keepdims=True), 1e-30)


def split_cols(h, sizes):
    out, off = [], 0
    for n in sizes:
        out.append(h[..., off:off + n])
        off += n
    return out


def to_blocks(a):
    b, t = a.shape[:2]
    return jnp.moveaxis(a.reshape((b, t // Q_BLOCK, Q_BLOCK) + a.shape[2:]), 1, 0)


def from_blocks(a):
    nb, b, q = a.shape[:3]
    return jnp.moveaxis(a, 0, 1).reshape((b, nb * q) + a.shape[3:])


def paged_rows(cache, layer, page_table):
    rows = cache[layer, page_table]
    return rows.reshape((rows.shape[0], rows.shape[1] * rows.shape[2]) + rows.shape[3:])


def sqrelu_mlp(x, w_up, w_down):
    return jnp.square(jax.nn.relu(x @ w_up)) @ w_down


def dsa_attend(q, iq, iw, q_pos, k, v, ik, k_pos, topk):
    b, tq = q.shape[:2]
    score = jnp.einsum('bthd,bsd->bths', iq, ik, preferred_element_type=F32) * (IDX_DIM ** -0.5)
    score = jnp.einsum('bths,bth->bts', jax.nn.relu(score), iw.astype(F32)) * (IDX_HEADS ** -0.5)
    causal = k_pos[None, :] <= q_pos[:, None]
    score = jnp.where(causal[None], score, -jnp.inf)
    _, idx = lax.top_k(score, topk)
    valid = jnp.take(k_pos, idx) <= q_pos[None, :, None]
    gather = jax.vmap(lambda rows, i: rows[i])
    kg, vg = gather(k, idx), gather(v, idx)
    qg = q.reshape(b, tq, A_KV, A_HEADS // A_KV, HEAD_DIM)
    s = jnp.einsum('btgjd,btkgd->btgjk', qg, kg, preferred_element_type=F32) * (HEAD_DIM ** -0.5)
    p = masked_softmax(s, valid[:, :, None, None, :])
    o = jnp.einsum('btgjk,btkgd->btgjd', p.astype(vg.dtype), vg)
    return o.reshape(b, tq, A_HEADS * HEAD_DIM)


def compress(k, v, pos_emb, w_ck, w_cv):
    b, l, g, d = k.shape
    nc = (l - CMP_LEN) // CMP_STRIDE + 1
    idx = jnp.arange(nc)[:, None] * CMP_STRIDE + jnp.arange(CMP_LEN)[None, :]
    pe = pos_emb[None, None, :, None, :]

    def phi(rows, w):
        blk = jnp.moveaxis(rows[:, idx] + pe, 3, 2).reshape(b, nc, g, CMP_LEN * d)
        return blk @ w

    c_end = jnp.arange(nc) * CMP_STRIDE + CMP_LEN - 1
    return phi(k, w_ck), phi(v, w_cv), c_end


def sel_blocks(k):
    b, l, g, d = k.shape
    ns = -(-l // SEL_LEN)
    k = jnp.pad(k, ((0, 0), (0, ns * SEL_LEN - l), (0, 0), (0, 0)))
    return jnp.moveaxis(k.reshape(b, ns, SEL_LEN, g, d), 3, 1)


def nsa_attend(q, gates, q_pos, ck, cv, c_end, sk, sv, wk, wv, w_pos):
    b, tq = q.shape[:2]
    nj = B_HEADS // B_KV
    scale = HEAD_DIM ** -0.5
    qg = q.reshape(b, tq, B_KV, nj, HEAD_DIM)
    s_c = jnp.einsum('btgjd,bcgd->btgjc', qg, ck, preferred_element_type=F32) * scale
    p_c = masked_softmax(s_c, (c_end[None, :] <= q_pos[:, None])[None, :, None, None, :])
    o_c = jnp.einsum('btgjc,bcgd->btgjd', p_c.astype(cv.dtype), cv)
    nc, ns = ck.shape[1], sk.shape[2]
    r, m = SEL_LEN // CMP_STRIDE, CMP_LEN // CMP_STRIDE
    off = (jnp.arange(r)[:, None] + jnp.arange(m)[None, :]).reshape(-1)
    ci = jnp.arange(ns)[:, None] * r - m + 1 + off[None, :]
    ok = (ci >= 0) & (ci < nc)
    imp_c = jnp.sum(p_c, axis=3)
    imp = jnp.sum(jnp.take(imp_c, jnp.clip(ci, 0, nc - 1), axis=-1) * ok, axis=-1)
    blk = jnp.arange(ns)[None, :]
    cur = (q_pos // SEL_LEN)[:, None]
    adm = blk * SEL_LEN <= q_pos[:, None]
    forced = adm & ((blk == 0) | (blk == cur) | (blk == cur - 1))
    score = jnp.where(forced[None, :, None, :], FORCE_SCORE,
                      jnp.where(adm[None, :, None, :], imp, -jnp.inf))
    _, bi = lax.top_k(score, min(SEL_TOP, ns))
    bi = jnp.moveaxis(bi, 2, 1)
    gather = jax.vmap(jax.vmap(lambda blocks, i: blocks[i]))
    kg, vg = gather(sk, bi), gather(sv, bi)
    n_sel = bi.shape[-1]
    kpos = bi[..., None] * SEL_LEN + jnp.arange(SEL_LEN)
    mask_s = (kpos <= q_pos[None, None, :, None, None]).reshape(b, B_KV, tq, 1, n_sel * SEL_LEN)
    s_s = jnp.einsum('btgjd,bgtnld->bgtjnl', qg, kg, preferred_element_type=F32) * scale
    p_s = masked_softmax(s_s.reshape(b, B_KV, tq, nj, n_sel * SEL_LEN), mask_s)
    p_s = p_s.reshape(b, B_KV, tq, nj, n_sel, SEL_LEN)
    o_s = jnp.einsum('bgtjnl,bgtnld->btgjd', p_s.astype(vg.dtype), vg)
    s_w = jnp.einsum('btgjd,bsgd->btgjs', qg, wk, preferred_element_type=F32) * scale
    dlt = q_pos[:, None] - w_pos[None, :]
    m_w = (dlt >= 0) & (dlt <= WINDOW) & (w_pos[None, :] >= 0)
    p_w = masked_softmax(s_w, m_w[None, :, None, None, :])
    o_w = jnp.einsum('btgjs,bsgd->btgjd', p_w.astype(wv.dtype), wv)
    g = gates.reshape(b, tq, B_KV, nj, 3)
    o = g[..., 0:1] * o_c + g[..., 1:2] * o_s + g[..., 2:3] * o_w
    return o.reshape(b, tq, B_HEADS * HEAD_DIM)


def fox_attend(q, cq, q_pos, segments):
    scale = HEAD_DIM ** -0.5
    cq_t = jnp.moveaxis(cq, 1, 2)[..., :, None]
    logits, masks = [], []
    for k, _, ck, k_pos in segments:
        s = jnp.einsum('bthd,bshd->bhts', q, k, preferred_element_type=F32) * scale
        logits.append(s + (cq_t - jnp.moveaxis(ck, 1, 2)[..., None, :]))
        masks.append(k_pos[None, :] <= q_pos[:, None])
    p = masked_softmax(jnp.concatenate(logits, axis=-1), jnp.concatenate(masks, axis=-1)[None, None])
    outs, off = [], 0
    for _, v, _, k_pos in segments:
        n = k_pos.shape[0]
        outs.append(jnp.einsum('bhts,bshd->bthd', p[..., off:off + n].astype(v.dtype), v))
        off += n
    o = sum(outs[1:], outs[0])
    return o.reshape(o.shape[0], o.shape[1], C_HEADS * HEAD_DIM)


def even_project(xn, pos, w_in, qn_a, kn_a, qn_b, kn_cmp, kn_sel, kn_win):
    b, t, _ = xn.shape
    (qa, ka, va, iq, ik, iw, qb, ck, cv, sk, sv, wk, wv, gl) = split_cols(xn @ w_in, EVEN_SIZES)
    heads = lambda a: a.reshape(b, t, -1, HEAD_DIM)
    qa = rope(rms_norm(heads(qa), qn_a), pos)
    ka = rope(rms_norm(heads(ka), kn_a), pos)
    iq = rope(iq.reshape(b, t, IDX_HEADS, IDX_DIM), pos)
    ik = rope(ik.reshape(b, t, 1, IDX_DIM), pos).reshape(b, t, IDX_DIM)
    qb = rope(rms_norm(heads(qb), qn_b), pos)
    ck = rope(rms_norm(heads(ck), kn_cmp), pos)
    sk = rope(rms_norm(heads(sk), kn_sel), pos)
    wk = rope(rms_norm(heads(wk), kn_win), pos)
    gates = jax.nn.sigmoid(gl).reshape(b, t, B_HEADS, 3)
    return (qa, ka, heads(va), iq, ik, iw, qb, ck, heads(cv), sk, heads(sv), wk, heads(wv), gates)


def even_mixer_prompt(xn, w_in, qn_a, kn_a, qn_b, kn_cmp, kn_sel, kn_win, cmp_pos, w_cmp_k, w_cmp_v):
    b, t, _ = xn.shape
    pos = jnp.arange(t)
    (qa, ka, va, iq, ik, iw, qb, ck, cv, sk, sv, wk, wv, gates) = even_project(
        xn, pos, w_in, qn_a, kn_a, qn_b, kn_cmp, kn_sel, kn_win)
    topk = min(DSA_TOPK, t // 4)
    cmp_k, cmp_v, c_end = compress(ck, cv, cmp_pos, w_cmp_k, w_cmp_v)
    sk_b, sv_b = sel_blocks(sk), sel_blocks(sv)
    pad = ((0, 0), (WINDOW, 0), (0, 0), (0, 0))
    wk_p, wv_p = jnp.pad(wk, pad), jnp.pad(wv, pad)
    band = WINDOW + Q_BLOCK

    def block(args):
        i, qa_i, iq_i, iw_i, qb_i, g_i = args
        start = i * Q_BLOCK
        q_pos = start + jnp.arange(Q_BLOCK)
        o_a = dsa_attend(qa_i, iq_i, iw_i, q_pos, ka, va, ik, pos, topk)
        wk_i = lax.dynamic_slice_in_dim(wk_p, start, band, axis=1)
        wv_i = lax.dynamic_slice_in_dim(wv_p, start, band, axis=1)
        w_pos = start - WINDOW + jnp.arange(band)
        o_b = nsa_attend(qb_i, g_i, q_pos, cmp_k, cmp_v, c_end, sk_b, sv_b, wk_i, wv_i, w_pos)
        return jnp.concatenate([o_a, o_b], axis=-1)

    o = from_blocks(lax.map(block, (jnp.arange(t // Q_BLOCK), to_blocks(qa), to_blocks(iq),
                                    to_blocks(iw), to_blocks(qb), to_blocks(gates))))
    n_keep = min(WINDOW, t)
    return o, (ka, va, ik, ck, cv, sk, sv, wk[:, t - n_keep:], wv[:, t - n_keep:])


def even_mixer_sample(xn, layer, page_table, cache_a_k, cache_a_v, cache_a_idx, cache_b_cmp_k,
                      cache_b_cmp_v, cache_b_sel_k, cache_b_sel_v, state_b_win_k, state_b_win_v,
                      w_in, qn_a, kn_a, qn_b, kn_cmp, kn_sel, kn_win, cmp_pos, w_cmp_k, w_cmp_v):
    t = xn.shape[1]
    past_len = page_table.shape[1] * PAGE_SIZE
    pos = past_len + jnp.arange(t)
    (qa, ka, va, iq, ik, iw, qb, ck, cv, sk, sv, wk, wv, gates) = even_project(
        xn, pos, w_in, qn_a, kn_a, qn_b, kn_cmp, kn_sel, kn_win)
    with_past = lambda cache, new: jnp.concatenate([paged_rows(cache, layer, page_table), new], axis=1)
    ka_all, va_all = with_past(cache_a_k, ka), with_past(cache_a_v, va)
    ik_all = with_past(cache_a_idx, ik)
    n_keys = ka_all.shape[1]
    o_a = dsa_attend(qa, iq, iw, pos, ka_all, va_all, ik_all, jnp.arange(n_keys),
                     min(DSA_TOPK, n_keys // 4))
    cmp_k, cmp_v, c_end = compress(with_past(cache_b_cmp_k, ck), with_past(cache_b_cmp_v, cv),
                                   cmp_pos, w_cmp_k, w_cmp_v)
    sk_b = sel_blocks(with_past(cache_b_sel_k, sk))
    sv_b = sel_blocks(with_past(cache_b_sel_v, sv))
    wk_all = jnp.concatenate([state_b_win_k[layer], wk], axis=1)
    wv_all = jnp.concatenate([state_b_win_v[layer], wv], axis=1)
    n_buf = state_b_win_k.shape[2]
    w_pos = past_len - n_buf + jnp.arange(n_buf + t)
    o_b = nsa_attend(qb, gates, pos, cmp_k, cmp_v, c_end, sk_b, sv_b, wk_all, wv_all, w_pos)
    n_keep = min(WINDOW, n_buf + t)
    o = jnp.concatenate([o_a, o_b], axis=-1)
    return o, (ka, va, ik, ck, cv, sk, sv, wk_all[:, n_buf + t - n_keep:], wv_all[:, n_buf + t - n_keep:])


def odd_project(xn, w_in, b_f, qn, kn):
    b, t, _ = xn.shape
    q, k, v, fl = split_cols(xn @ w_in, ODD_SIZES)
    heads = lambda a: a.reshape(b, t, C_HEADS, HEAD_DIM)
    logf = jax.nn.log_sigmoid(fl.astype(F32) + b_f.astype(F32)).astype(xn.dtype)
    return rms_norm(heads(q), qn), rms_norm(heads(k), kn), heads(v), logf


def odd_mixer_prompt(xn, w_in, b_f, qn, kn):
    t = xn.shape[1]
    q, k, v, logf = odd_project(xn, w_in, b_f, qn, kn)
    pos = jnp.arange(t)
    cum = jnp.cumsum(logf.astype(F32), axis=1)

    def block(args):
        i, q_i, c_i = args
        return fox_attend(q_i, c_i, i * Q_BLOCK + jnp.arange(Q_BLOCK), [(k, v, cum, pos)])

    o = from_blocks(lax.map(block, (jnp.arange(t // Q_BLOCK), to_blocks(q), to_blocks(cum))))
    return o, (k, v, logf)


def odd_mixer_sample(xn, layer, page_table, cache_c_k, cache_c_v, cache_c_logf, w_in, b_f, qn, kn):
    t = xn.shape[1]
    past_len = page_table.shape[1] * PAGE_SIZE
    q, k, v, logf = odd_project(xn, w_in, b_f, qn, kn)
    pos = past_len + jnp.arange(t)
    logf_all = jnp.concatenate([paged_rows(cache_c_logf, layer, page_table), logf], axis=1)
    cum = jnp.cumsum(logf_all.astype(F32), axis=1)
    segs = [(paged_rows(cache_c_k, layer, page_table), paged_rows(cache_c_v, layer, page_table),
             cum[:, :past_len], jnp.arange(past_len)),
            (k, v, cum[:, past_len:], pos)]
    return fox_attend(q, cum[:, past_len:], pos, segs), (k, v, logf)


def setup_inputs(seed: int = 0) -> dict:
    key = jax.random.key(seed)
    kit = iter(list(jax.random.split(key, 64)))
    nrm = lambda shape, scale=1.0: scale * jax.random.normal(next(kit), shape, F32)
    gain = lambda shape: 1.0 + 0.05 * jax.random.normal(next(kit), shape, F32)
    n_pages = PAST_LEN // PAGE_SIZE
    n_used = DEC_BATCH * n_pages
    n_pool = n_used + (n_used + 3) // 4
    n_buf = min(WINDOW, PAST_LEN)
    pg = (n_pool, PAGE_SIZE)
    inp = {}
    inp['x_prompt'] = nrm((BATCH, SEQ, D_MODEL))
    inp['x_sample'] = nrm((DEC_BATCH, DEC_SEQ, D_MODEL))
    inp['cache_a_k'] = nrm((N_EVEN,) + pg + (A_KV, HEAD_DIM))
    inp['cache_a_v'] = nrm((N_EVEN,) + pg + (A_KV, HEAD_DIM))
    inp['cache_a_idx'] = nrm((N_EVEN,) + pg + (IDX_DIM,))
    inp['cache_b_cmp_k'] = nrm((N_EVEN,) + pg + (B_KV, HEAD_DIM))
    inp['cache_b_cmp_v'] = nrm((N_EVEN,) + pg + (B_KV, HEAD_DIM))
    inp['cache_b_sel_k'] = nrm((N_EVEN,) + pg + (B_KV, HEAD_DIM))
    inp['cache_b_sel_v'] = nrm((N_EVEN,) + pg + (B_KV, HEAD_DIM))
    inp['state_b_win_k'] = nrm((N_EVEN, DEC_BATCH, n_buf, B_KV, HEAD_DIM))
    inp['state_b_win_v'] = nrm((N_EVEN, DEC_BATCH, n_buf, B_KV, HEAD_DIM))
    inp['cache_c_k'] = nrm((N_ODD,) + pg + (C_HEADS, HEAD_DIM))
    inp['cache_c_v'] = nrm((N_ODD,) + pg + (C_HEADS, HEAD_DIM))
    inp['cache_c_logf'] = jax.nn.log_sigmoid(2.0 + nrm((N_ODD,) + pg + (C_HEADS,), 0.5))
    inp['page_table'] = jax.random.permutation(next(kit), n_pool)[:n_used].reshape(
        DEC_BATCH, n_pages).astype(jnp.int32)
    inp['norm_mix'] = gain((DEPTH, D_MODEL))
    inp['norm_ffn'] = gain((DEPTH, D_MODEL))
    inp['e_w_in'] = nrm((N_EVEN, D_MODEL, EVEN_IN), D_MODEL ** -0.5)
    inp['e_qn_a'] = gain((N_EVEN, HEAD_DIM))
    inp['e_kn_a'] = gain((N_EVEN, HEAD_DIM))
    inp['e_qn_b'] = gain((N_EVEN, HEAD_DIM))
    inp['e_kn_cmp'] = gain((N_EVEN, HEAD_DIM))
    inp['e_kn_sel'] = gain((N_EVEN, HEAD_DIM))
    inp['e_kn_win'] = gain((N_EVEN, HEAD_DIM))
    inp['e_cmp_pos'] = nrm((N_EVEN, CMP_LEN, HEAD_DIM), 0.2)
    inp['e_w_cmp_k'] = nrm((N_EVEN, CMP_LEN * HEAD_DIM, HEAD_DIM), (CMP_LEN * HEAD_DIM) ** -0.5)
    inp['e_w_cmp_v'] = nrm((N_EVEN, CMP_LEN * HEAD_DIM, HEAD_DIM), (CMP_LEN * HEAD_DIM) ** -0.5)
    inp['e_w_out'] = nrm((N_EVEN, (A_HEADS + B_HEADS) * HEAD_DIM, D_MODEL), D_MODEL ** -0.5)
    inp['o_w_in'] = nrm((N_ODD, D_MODEL, ODD_IN), D_MODEL ** -0.5)
    inp['o_b_f'] = 2.0 + nrm((N_ODD, C_HEADS), 0.1)
    inp['o_qn'] = gain((N_ODD, HEAD_DIM))
    inp['o_kn'] = gain((N_ODD, HEAD_DIM))
    inp['o_w_out'] = nrm((N_ODD, C_HEADS * HEAD_DIM, D_MODEL), D_MODEL ** -0.5)
    inp['w_up'] = nrm((DEPTH, D_MODEL, D_FF), D_MODEL ** -0.5)
    inp['w_down'] = nrm((DEPTH, D_FF, D_MODEL), D_FF ** -0.5)
    return inp


def reference(x_prompt, x_sample, cache_a_k, cache_a_v, cache_a_idx, cache_b_cmp_k, cache_b_cmp_v,
              cache_b_sel_k, cache_b_sel_v, state_b_win_k, state_b_win_v, cache_c_k, cache_c_v,
              cache_c_logf, page_table, norm_mix, norm_ffn, e_w_in, e_qn_a, e_kn_a, e_qn_b, e_kn_cmp,
              e_kn_sel, e_kn_win, e_cmp_pos, e_w_cmp_k, e_w_cmp_v, e_w_out, o_w_in, o_b_f, o_qn, o_kn,
              o_w_out, w_up, w_down):
    xp, xs = x_prompt, x_sample
    ev_p, ev_s, od_p, od_s = [], [], [], []
    for li in range(DEPTH):
        j = li // 2
        hp, hs = rms_norm(xp, norm_mix[li]), rms_norm(xs, norm_mix[li])
        if li % 2 == 0:
            ew = (e_w_in[j], e_qn_a[j], e_kn_a[j], e_qn_b[j], e_kn_cmp[j], e_kn_sel[j], e_kn_win[j],
                  e_cmp_pos[j], e_w_cmp_k[j], e_w_cmp_v[j])
            op, st_p = even_mixer_prompt(hp, *ew)
            osmp, st_s = even_mixer_sample(hs, j, page_table, cache_a_k, cache_a_v, cache_a_idx,
                                           cache_b_cmp_k, cache_b_cmp_v, cache_b_sel_k, cache_b_sel_v,
                                           state_b_win_k, state_b_win_v, *ew)
            ev_p.append(st_p)
            ev_s.append(st_s)
            w_out = e_w_out[j]
        else:
            ow = (o_w_in[j], o_b_f[j], o_qn[j], o_kn[j])
            op, st_p = odd_mixer_prompt(hp, *ow)
            osmp, st_s = odd_mixer_sample(hs, j, page_table, cache_c_k, cache_c_v, cache_c_logf, *ow)
            od_p.append(st_p)
            od_s.append(st_s)
            w_out = o_w_out[j]
        xp = xp + op @ w_out
        xs = xs + osmp @ w_out
        xp = xp + sqrelu_mlp(rms_norm(xp, norm_ffn[li]), w_up[li], w_down[li])
        xs = xs + sqrelu_mlp(rms_norm(xs, norm_ffn[li]), w_up[li], w_down[li])
    stk = lambda states, n: jnp.stack([s[n] for s in states])
    return (xp, xs,
            stk(ev_p, 0), stk(ev_s, 0), stk(ev_p, 1), stk(ev_s, 1), stk(ev_p, 2), stk(ev_s, 2),
            stk(ev_p, 3), stk(ev_s, 3), stk(ev_p, 4), stk(ev_s, 4), stk(ev_p, 5), stk(ev_s, 5),
            stk(ev_p, 6), stk(ev_s, 6), stk(ev_p, 7), stk(ev_s, 7), stk(ev_p, 8), stk(ev_s, 8),
            stk(od_p, 0), stk(od_s, 0), stk(od_p, 1), stk(od_s, 1), stk(od_p, 2), stk(od_s, 2))
```

```python
import functools

import numpy as np
import jax
import jax.numpy as jnp
from jax import lax
from jax.experimental import pallas as pl
from jax.experimental.pallas import tpu as pltpu

F32 = jnp.float32
BF16 = jnp.bfloat16

HEAD_DIM = 64
A_KV = 2
IDX_HEADS = 4
IDX_DIM = 64
DSA_TOPK = 256
B_KV = 2
CMP_LEN = 32
CMP_STRIDE = 16
SEL_LEN = 64
SEL_TOP = 16
WINDOW = 512
ROPE_THETA = 10000.0
RMS_EPS = 1e-6
NEG_BIG = -1e30
FORCE_SCORE = 1e9
PAGE_SIZE = 128

LANES = 128
VMEM_LIMIT = 56 * 1024 * 1024

PLAIN, ROPE, NORM_ROPE, NORM, LOGSIG = range(5)

INT_MIN = np.int32(-2 ** 31)
KEY_NINF = np.int32(-2139095041)


def _dot_nt(a, b):
    return lax.dot_general(a, b, (((1,), (1,)), ((), ())), preferred_element_type=F32)


def _iota(shape, dim):
    return lax.broadcasted_iota(jnp.int32, shape, dim)


def _round_up(x, m):
    return -(-x // m) * m


def _div_pow2(x, n):
    assert n & (n - 1) == 0
    return lax.shift_right_arithmetic(x, jnp.int32(n.bit_length() - 1))


def _inproj_body(x_ref, g_ref, w_ref, gains_ref, cos_ref, sin_ref, bias_ref, o32_ref, o16_ref, *, kinds):
    x = x_ref[...]
    xn = x * lax.rsqrt(jnp.mean(x * x, axis=-1, keepdims=True) + RMS_EPS) * g_ref[...]
    xb = xn.astype(BF16)
    cos = cos_ref[...]
    sin = sin_ref[...]
    lane = _iota((1, LANES), 1)
    lo = lane < HEAD_DIM
    first_half = (lane & (HEAD_DIM - 1)) < (HEAD_DIM // 2)
    n_chunks = len(kinds)
    group = 4
    for c0 in range(0, n_chunks, group):
        c1 = min(c0 + group, n_chunks)
        hh = jnp.dot(xb, w_ref[:, c0 * LANES:c1 * LANES], preferred_element_type=F32)
        for c in range(c0, c1):
            h = hh[:, (c - c0) * LANES:(c - c0 + 1) * LANES]
            kind, gi = kinds[c]
            if kind in (NORM_ROPE, NORM):
                ss = h * h
                s_lo = jnp.sum(jnp.where(lo, ss, 0.0), axis=-1, keepdims=True)
                s_hi = jnp.sum(jnp.where(lo, 0.0, ss), axis=-1, keepdims=True)
                ms = jnp.where(lo, s_lo, s_hi) * (1.0 / HEAD_DIM)
                h = h * lax.rsqrt(ms + RMS_EPS) * gains_ref[gi:gi + 1, :]
            if kind in (NORM_ROPE, ROPE):
                partner = jnp.where(first_half, pltpu.roll(h, LANES - HEAD_DIM // 2, 1),
                                    pltpu.roll(h, HEAD_DIM // 2, 1))
                h = h * cos + partner * sin
            if kind == LOGSIG:
                z = h + bias_ref[...]
                h = -(jnp.maximum(-z, 0.0) + jnp.log1p(jnp.exp(-jnp.abs(z))))
            o32_ref[:, c * LANES:(c + 1) * LANES] = h
            o16_ref[:, c * LANES:(c + 1) * LANES] = h.astype(BF16)


def _inproj(x, g, w, gains, cos, sin, bias, kinds, tm):
    n, d = x.shape
    nout = w.shape[1]
    n_tab = cos.shape[0] // tm
    const = lambda i: (0, 0)
    return pl.pallas_call(
        functools.partial(_inproj_body, kinds=kinds),
        grid=(n // tm,),
        in_specs=[
            pl.BlockSpec((tm, d), lambda i: (i, 0)),
            pl.BlockSpec((1, d), const),
            pl.BlockSpec((d, nout), const, pipeline_mode=pl.Buffered(1)),
            pl.BlockSpec(gains.shape, const),
            pl.BlockSpec((tm, LANES), lambda i: (i % n_tab, 0)),
            pl.BlockSpec((tm, LANES), lambda i: (i % n_tab, 0)),
            pl.BlockSpec((1, LANES), const),
        ],
        out_specs=[pl.BlockSpec((tm, nout), lambda i: (i, 0)),
                   pl.BlockSpec((tm, nout), lambda i: (i, 0))],
        out_shape=[jax.ShapeDtypeStruct((n, nout), F32), jax.ShapeDtypeStruct((n, nout), BF16)],
        compiler_params=pltpu.CompilerParams(dimension_semantics=("parallel",), vmem_limit_bytes=VMEM_LIMIT),
        name="inproj",
    )(x, g.reshape(1, d), w, gains, cos, sin, bias)


def _postmix_body(*refs, n_o, ff_chunk):
    x_ref = refs[0]
    o_refs = refs[1:1 + n_o]
    w_ref, g_ref, wup_ref, wdn_ref, out_ref = refs[1 + n_o:]
    o = o_refs[0][...] if n_o == 1 else jnp.concatenate([r[...] for r in o_refs], axis=-1)
    x1 = x_ref[...] + jnp.dot(o, w_ref[...], preferred_element_type=F32)
    xn = x1 * lax.rsqrt(jnp.mean(x1 * x1, axis=-1, keepdims=True) + RMS_EPS) * g_ref[...]
    xb = xn.astype(BF16)
    acc = x1
    d_ff = wup_ref.shape[1]
    for c in range(d_ff // ff_chunk):
        h = jnp.dot(xb, wup_ref[:, c * ff_chunk:(c + 1) * ff_chunk], preferred_element_type=F32)
        a = jnp.square(jnp.maximum(h, 0.0)).astype(BF16)
        acc = acc + jnp.dot(a, wdn_ref[c * ff_chunk:(c + 1) * ff_chunk, :], preferred_element_type=F32)
    out_ref[...] = acc


def _postmix(x, os_, ws, g, w_up, w_down, tm):
    n, d = x.shape
    const = lambda i: (0, 0)
    row = lambda i: (i, 0)
    w_out = ws[0] if len(ws) == 1 else jnp.concatenate(ws, axis=0)
    in_specs = [pl.BlockSpec((tm, d), row)]
    in_specs += [pl.BlockSpec((tm, o.shape[1]), row) for o in os_]
    in_specs += [pl.BlockSpec(w_out.shape, const, pipeline_mode=pl.Buffered(1)),
                 pl.BlockSpec((1, d), const),
                 pl.BlockSpec(w_up.shape, const, pipeline_mode=pl.Buffered(1)),
                 pl.BlockSpec(w_down.shape, const, pipeline_mode=pl.Buffered(1))]
    return pl.pallas_call(
        functools.partial(_postmix_body, n_o=len(os_), ff_chunk=1024),
        grid=(n // tm,),
        in_specs=in_specs,
        out_specs=pl.BlockSpec((tm, d), row),
        out_shape=jax.ShapeDtypeStruct((n, d), F32),
        compiler_params=pltpu.CompilerParams(dimension_semantics=("parallel",), vmem_limit_bytes=VMEM_LIMIT),
        name="postmix",
    )(x, *os_, w_out, g.reshape(1, d), w_up, w_down)


def _gather_body(pt_ref, *refs, n, n_pages, dup):
    del pt_ref
    p = pl.program_id(1)
    for a in range(n):
        page = refs[a][...]
        new = refs[n + a][...]
        out_ref = refs[2 * n + a]
        val = jnp.where(p < n_pages, page, jnp.where(p == n_pages, new, 0.0))
        if dup[a]:
            val = jnp.concatenate([val, val], axis=-1)
        out_ref[...] = val.astype(out_ref.dtype)


def _gather(page_table, caches, layer, news, dup, out_dtypes, n_blocks):
    bs, n_pages = page_table.shape
    n = len(caches)
    in_specs, out_specs, out_shape = [], [], []
    for c in caches:
        in_specs.append(pl.BlockSpec(
            (None, None, PAGE_SIZE, c.shape[-1]),
            lambda b, p, pt: (layer, pt[b, jnp.minimum(p, n_pages - 1)], 0, 0)))
    for nw in news:
        in_specs.append(pl.BlockSpec((None, PAGE_SIZE, nw.shape[-1]), lambda b, p, pt: (b, 0, 0)))
    for c, d_, dt in zip(caches, dup, out_dtypes):
        width = c.shape[-1] * (2 if d_ else 1)
        out_specs.append(pl.BlockSpec((None, PAGE_SIZE, width), lambda b, p, pt: (b, p, 0)))
        out_shape.append(jax.ShapeDtypeStruct((bs, n_blocks * PAGE_SIZE, width), dt))
    return pl.pallas_call(
        functools.partial(_gather_body, n=n, n_pages=n_pages, dup=tuple(dup)),
        grid_spec=pltpu.PrefetchScalarGridSpec(
            num_scalar_prefetch=1, grid=(bs, n_blocks), in_specs=in_specs, out_specs=out_specs),
        out_shape=out_shape,
        compiler_params=pltpu.CompilerParams(dimension_semantics=("parallel", "arbitrary")),
        name="paged_gather",
    )(page_table, *caches, *news)


def _float_key(s):
    b = lax.bitcast_convert_type(s + 0.0, jnp.int32)
    return jnp.where(b >= 0, b, b ^ jnp.int32(0x7FFFFFFF))


def _prefix_tri():
    return jnp.where(_iota((LANES, LANES), 0) <= _iota((LANES, LANES), 1), 1.0, 0.0).astype(BF16)


def _topk_mask(key_ref, sel_ref, k, tri):
    n_ch, rows, width = key_ref.shape
    pieces = [(c, u) for c in range(n_ch) for u in range(width // LANES)]

    def count(pred):
        acc = jnp.zeros((rows, LANES), F32)
        for c, u in pieces:
            acc = acc + jnp.where(pred(key_ref[c, :, u * LANES:(u + 1) * LANES]), 1.0, 0.0)
        return jnp.sum(acc, axis=-1, keepdims=True)

    kf = float(k)
    zero = jnp.zeros((rows, 1), jnp.int32)
    t = jnp.where(count(lambda kc: kc >= zero) >= kf, zero, jnp.full((rows, 1), INT_MIN, jnp.int32))

    def bit_step(i, t):
        cand = t | jnp.left_shift(jnp.int32(1), 30 - i)
        return jnp.where(count(lambda kc: kc >= cand) >= kf, cand, t)

    t = lax.fori_loop(0, 31, bit_step, t)
    need = kf - count(lambda kc: kc > t)
    carry = jnp.zeros((rows, 1), F32)
    for c, u in pieces:
        kc = key_ref[c, :, u * LANES:(u + 1) * LANES]
        eq = kc == t
        pc = jnp.dot(jnp.where(eq, 1.0, 0.0).astype(BF16), tri, preferred_element_type=F32) + carry
        sel = ((kc > t) | (eq & (pc <= need))) & (kc > KEY_NINF)
        sel_ref[c, :, u * LANES:(u + 1) * LANES] = jnp.where(sel, 1.0, 0.0)
        carry = pc[:, LANES - 1:LANES]


def _flash_pair(q2, k_ref, v_ref, c_lo, c_hi, ch, mask_fn, bias_fn, scale):
    tq = q2.shape[0]
    lo = _iota((1, LANES), 1) < HEAD_DIM
    zero = jnp.zeros_like(q2)
    qs = (jnp.where(lo, q2, zero), jnp.where(lo, zero, q2))

    def body(c, carry):
        off = pl.multiple_of(c * ch, ch)
        kc = k_ref[0, pl.ds(off, ch), :]
        vc = v_ref[0, pl.ds(off, ch), :]
        out = []
        for half in range(2):
            m, l, a = carry[3 * half:3 * half + 3]
            s = _dot_nt(qs[half], kc) * scale
            if bias_fn is not None:
                s = s + bias_fn(c, half)
            msk = mask_fn(c, half)
            s = jnp.where(msk, s, NEG_BIG)
            m_new = jnp.maximum(m, jnp.max(s, axis=-1, keepdims=True))
            alpha = jnp.exp(m - m_new)
            p = jnp.where(msk, jnp.exp(s - m_new), 0.0)
            l = alpha * l + jnp.sum(p, axis=-1, keepdims=True)
            a = alpha * a + jnp.dot(p.astype(BF16), vc, preferred_element_type=F32)
            out += [m_new, l, a]
        return tuple(out)

    init = (jnp.full((tq, 1), NEG_BIG, F32), jnp.zeros((tq, 1), F32), jnp.zeros((tq, LANES), F32)) * 2
    ma, la, aa, mb, lb, ab = lax.fori_loop(c_lo, c_hi, body, init)
    return jnp.where(lo, aa / jnp.maximum(la, 1e-30), ab / jnp.maximum(lb, 1e-30))


def _causal_chunks(q0, tq, n_ch, ch):
    return jnp.minimum(n_ch, _div_pow2(q0 + tq - 1, ch) + 1)


def _dsa_body(qa_ref, iq_ref, misc_ref, k_ref, v_ref, ik_ref, o_ref, key_ref, sel_ref,
              *, tq, ch, topk, qpos0, n_keys):
    n_ch = key_ref.shape[0]
    q0 = qpos0 + pl.program_id(1) * tq
    qpos = q0 + _iota((tq, 1), 0)
    c_hi = _causal_chunks(q0, tq, n_ch, ch)
    lo = _iota((1, LANES), 1) < HEAD_DIM
    iw = misc_ref[0][:, 0:IDX_HEADS]
    tri = _prefix_tri()

    def valid_fn(c):
        kpos = c * ch + _iota((1, ch), 1)
        return (kpos <= qpos) & (kpos < n_keys)

    key_ref[...] = jnp.full(key_ref.shape, KEY_NINF, jnp.int32)

    def score_body(c, _):
        off = pl.multiple_of(c * ch, ch)
        ikc = ik_ref[0, pl.ds(off, ch), :]
        acc = jnp.zeros((tq, ch), F32)
        for hp in range(IDX_HEADS // 2):
            iqc = iq_ref[0, :, hp * LANES:(hp + 1) * LANES]
            zero = jnp.zeros_like(iqc)
            for half in range(2):
                qm = jnp.where(lo, iqc, zero) if half == 0 else jnp.where(lo, zero, iqc)
                s = _dot_nt(qm, ikc) * (IDX_DIM ** -0.5)
                h = 2 * hp + half
                acc = acc + jnp.maximum(s, 0.0) * iw[:, h:h + 1]
        sc = acc * (IDX_HEADS ** -0.5)
        key_ref[c] = jnp.where(valid_fn(c), _float_key(sc), KEY_NINF)
        return 0

    lax.fori_loop(0, c_hi, score_body, 0)
    _topk_mask(key_ref, sel_ref, topk, tri)

    n_chunks_q = qa_ref.shape[2] // LANES
    for j in range(n_chunks_q):
        q2 = qa_ref[0, :, j * LANES:(j + 1) * LANES]
        o = _flash_pair(q2, k_ref, v_ref, 0, c_hi, ch, lambda c, half: sel_ref[c] > 0.5, None, HEAD_DIM ** -0.5)
        o_ref[0, :, j * LANES:(j + 1) * LANES] = o.astype(o_ref.dtype)


def _dsa(q16, m32, kv16, col, tq, topk, qpos0, n_keys):
    b, t, _ = q16.shape
    s = kv16.shape[1]
    ch = 512 if s % 512 == 0 else 256
    n_qa = 4 * LANES
    return pl.pallas_call(
        functools.partial(_dsa_body, tq=tq, ch=ch, topk=topk, qpos0=qpos0, n_keys=n_keys),
        grid=(b, t // tq),
        in_specs=[
            pl.BlockSpec((1, tq, n_qa), lambda bi, i: (bi, i, col["qa"] * LANES // n_qa)),
            pl.BlockSpec((1, tq, 2 * LANES), lambda bi, i: (bi, i, col["iq"] // 2)),
            pl.BlockSpec((1, tq, LANES), lambda bi, i: (bi, i, col["misc"])),
            pl.BlockSpec((1, s, LANES), lambda bi, i: (bi, 0, col["k"])),
            pl.BlockSpec((1, s, LANES), lambda bi, i: (bi, 0, col["v"])),
            pl.BlockSpec((1, s, LANES), lambda bi, i: (bi, 0, col["ik"])),
        ],
        out_specs=pl.BlockSpec((1, tq, n_qa), lambda bi, i: (bi, i, 0)),
        out_shape=jax.ShapeDtypeStruct((b, t, n_qa), BF16),
        scratch_shapes=[pltpu.VMEM((s // ch, tq, ch), jnp.int32), pltpu.VMEM((s // ch, tq, ch), F32)],
        compiler_params=pltpu.CompilerParams(dimension_semantics=("parallel", "arbitrary"),
                                             vmem_limit_bytes=VMEM_LIMIT),
        name="dsa",
    )(q16, q16, m32, kv16, kv16, kv16)


def _nsa_body(qb_ref, misc_ref, cmpk_ref, cmpv_ref, sk_ref, sv_ref, wk_ref, wv_ref, imp_ref, exp_ref,
              o_ref, key_ref, selb_ref, bm_ref,
              *, tq, ch, n_sel, qpos0, wpos0, n_wvalid):
    n_ch = exp_ref.shape[0]
    ncp = cmpk_ref.shape[1]
    nsp = imp_ref.shape[1]
    q0 = qpos0 + pl.program_id(1) * tq
    qpos = q0 + _iota((tq, 1), 0)
    c_hi = _causal_chunks(q0, tq, n_ch, ch)
    lo = _iota((1, LANES), 1) < HEAD_DIM
    scale = HEAD_DIM ** -0.5
    tri = _prefix_tri()
    n_chunks_q = qb_ref.shape[2] // LANES
    gates = jax.nn.sigmoid(misc_ref[0])

    cvalid = (_iota((1, ncp), 1) * CMP_STRIDE + (CMP_LEN - 1)) <= qpos
    ck = cmpk_ref[0]
    cv = cmpv_ref[0]
    imp_c = [jnp.zeros((tq, ncp), F32), jnp.zeros((tq, ncp), F32)]
    o_cmp = []
    for j in range(n_chunks_q):
        q2 = qb_ref[0, :, j * LANES:(j + 1) * LANES]
        zero = jnp.zeros_like(q2)
        outs = []
        for half in range(2):
            qm = jnp.where(lo, q2, zero) if half == 0 else jnp.where(lo, zero, q2)
            s = jnp.where(cvalid, _dot_nt(qm, ck) * scale, NEG_BIG)
            p = jnp.where(cvalid, jnp.exp(s - jnp.max(s, axis=-1, keepdims=True)), 0.0)
            pn = p / jnp.maximum(jnp.sum(p, axis=-1, keepdims=True), 1e-30)
            imp_c[half] = imp_c[half] + pn
            outs.append(jnp.dot(pn.astype(BF16), cv, preferred_element_type=F32))
        o_cmp.append(jnp.where(lo, outs[0], outs[1]))

    blk = _iota((1, nsp), 1)
    adm = blk * SEL_LEN <= qpos
    cur = _div_pow2(qpos, SEL_LEN)
    forced = adm & ((blk == 0) | (blk == cur) | (blk == cur - 1))
    for g in range(2):
        imp = jnp.dot(imp_c[g], imp_ref[...], preferred_element_type=F32, precision=lax.Precision.HIGHEST)
        score = jnp.where(forced, FORCE_SCORE, jnp.where(adm, imp, -jnp.inf))
        key_ref[0] = _float_key(score)
        _topk_mask(key_ref, selb_ref, n_sel, tri)
        bm_ref[g] = selb_ref[0]
    bm16 = (bm_ref[0].astype(BF16), bm_ref[1].astype(BF16))

    def sel_mask(c, half):
        kpos = c * ch + _iota((1, ch), 1)
        return (jnp.dot(bm16[half], exp_ref[c], preferred_element_type=F32) > 0.5) & (kpos <= qpos)

    wch = LANES
    n_wch = wk_ref.shape[1] // wch
    wc_lo = jnp.minimum(_div_pow2(jnp.maximum(q0 - WINDOW - wpos0, 0), wch), n_wch)
    wc_hi = jnp.clip(_div_pow2(q0 + tq - 1 - wpos0, wch) + 1, 0, n_wch)

    def win_mask(c, half):
        row = c * wch + _iota((1, wch), 1)
        kpos = wpos0 + row
        dlt = qpos - kpos
        return (dlt >= 0) & (dlt <= WINDOW) & (kpos >= 0) & (row < n_wvalid)

    for j in range(n_chunks_q):
        q2 = qb_ref[0, :, j * LANES:(j + 1) * LANES]
        o_sel = _flash_pair(q2, sk_ref, sv_ref, 0, c_hi, ch, sel_mask, None, scale)
        o_win = _flash_pair(q2, wk_ref, wv_ref, wc_lo, wc_hi, wch, win_mask, None, scale)
        o = jnp.zeros((tq, LANES), F32)
        for bi, ob in enumerate((o_cmp[j], o_sel, o_win)):
            c_a = IDX_HEADS + 3 * j + bi
            c_b = IDX_HEADS + 3 * (n_chunks_q + j) + bi
            o = o + jnp.where(lo, gates[:, c_a:c_a + 1], gates[:, c_b:c_b + 1]) * ob
        o_ref[0, :, j * LANES:(j + 1) * LANES] = o.astype(o_ref.dtype)


def _nsa(q16, m32, cmpk, cmpv, kv16, w16, imp_mat, exp_mat, col, tq, n_sel, qpos0, wpos0, n_wvalid):
    b, t, _ = q16.shape
    s = kv16.shape[1]
    sw = w16.shape[1]
    ch = exp_mat.shape[2]
    ncp, nsp = imp_mat.shape
    n_qb = 4 * LANES
    full2 = lambda bi, i: (0, 0)
    return pl.pallas_call(
        functools.partial(_nsa_body, tq=tq, ch=ch, n_sel=n_sel, qpos0=qpos0, wpos0=wpos0, n_wvalid=n_wvalid),
        grid=(b, t // tq),
        in_specs=[
            pl.BlockSpec((1, tq, n_qb), lambda bi, i: (bi, i, col["qb"] * LANES // n_qb)),
            pl.BlockSpec((1, tq, LANES), lambda bi, i: (bi, i, col["misc"])),
            pl.BlockSpec((1, ncp, LANES), lambda bi, i: (bi, 0, 0)),
            pl.BlockSpec((1, ncp, LANES), lambda bi, i: (bi, 0, 0)),
            pl.BlockSpec((1, s, LANES), lambda bi, i: (bi, 0, col["sk"])),
            pl.BlockSpec((1, s, LANES), lambda bi, i: (bi, 0, col["sv"])),
            pl.BlockSpec((1, sw, LANES), lambda bi, i: (bi, 0, col["wk"])),
            pl.BlockSpec((1, sw, LANES), lambda bi, i: (bi, 0, col["wv"])),
            pl.BlockSpec((ncp, nsp), full2),
            pl.BlockSpec(exp_mat.shape, lambda bi, i: (0, 0, 0)),
        ],
        out_specs=pl.BlockSpec((1, tq, n_qb), lambda bi, i: (bi, i, 0)),
        out_shape=jax.ShapeDtypeStruct((b, t, n_qb), BF16),
        scratch_shapes=[pltpu.VMEM((1, tq, nsp), jnp.int32), pltpu.VMEM((1, tq, nsp), F32),
                        pltpu.VMEM((2, tq, nsp), F32)],
        compiler_params=pltpu.CompilerParams(dimension_semantics=("parallel", "arbitrary"),
                                             vmem_limit_bytes=VMEM_LIMIT),
        name="nsa",
    )(q16, m32, cmpk, cmpv, kv16, kv16, w16, w16, imp_mat, exp_mat)


def _compress_body(uk_ref, uv_ref, wk_ref, wv_ref, pe_ref, ok_ref, ov_ref, shift_ref):
    nu = uk_ref.shape[1]
    half = wk_ref.shape[0] // 2
    shift_ref[nu:nu + 8, :] = jnp.zeros((8, LANES), F32)
    for u_ref, w_ref, o_ref in ((uk_ref, wk_ref, ok_ref), (uv_ref, wv_ref, ov_ref)):
        u = u_ref[0]
        top = jnp.dot(u, w_ref[0:half, :], preferred_element_type=F32)
        shift_ref[0:nu, :] = jnp.dot(u, w_ref[half:2 * half, :], preferred_element_type=F32)
        pec = jnp.dot(pe_ref[...], w_ref[...], preferred_element_type=F32)[0:1, :]
        o_ref[0] = (top + shift_ref[1:nu + 1, :] + pec).astype(o_ref.dtype)


def _compress(uk, uv, wk2, wv2, pe2):
    b, nu, width = uk.shape
    const = lambda bi: (0, 0)
    row = lambda bi: (bi, 0, 0)
    return pl.pallas_call(
        _compress_body,
        grid=(b,),
        in_specs=[pl.BlockSpec((1, nu, width), row), pl.BlockSpec((1, nu, width), row),
                  pl.BlockSpec(wk2.shape, const), pl.BlockSpec(wv2.shape, const), pl.BlockSpec(pe2.shape, const)],
        out_specs=[pl.BlockSpec((1, nu, LANES), row), pl.BlockSpec((1, nu, LANES), row)],
        out_shape=[jax.ShapeDtypeStruct((b, nu, LANES), BF16)] * 2,
        scratch_shapes=[pltpu.VMEM((nu + 8, LANES), F32)],
        compiler_params=pltpu.CompilerParams(dimension_semantics=("parallel",), vmem_limit_bytes=VMEM_LIMIT),
        name="compress",
    )(uk, uv, wk2, wv2, pe2)


def _cumsum_body(x_ref, o_ref):
    s = x_ref.shape[2]
    tri = jnp.where(_iota((LANES, LANES), 0) <= _iota((LANES, LANES), 1), 1.0, 0.0)
    carry = jnp.zeros((x_ref.shape[1], 1), F32)
    for c in range(s // LANES):
        blk = x_ref[0, :, c * LANES:(c + 1) * LANES]
        pc = jnp.dot(blk, tri, preferred_element_type=F32, precision=lax.Precision.HIGHEST) + carry
        o_ref[0, :, c * LANES:(c + 1) * LANES] = pc
        carry = pc[:, LANES - 1:LANES]


def _cumsum(x):
    b, h, s = x.shape
    return pl.pallas_call(
        _cumsum_body,
        grid=(b,),
        in_specs=[pl.BlockSpec((1, h, s), lambda bi: (bi, 0, 0))],
        out_specs=pl.BlockSpec((1, h, s), lambda bi: (bi, 0, 0)),
        out_shape=jax.ShapeDtypeStruct((b, h, s), F32),
        compiler_params=pltpu.CompilerParams(dimension_semantics=("parallel",)),
        name="cumsum",
    )(x)


def _fox_body(q_ref, k_ref, v_ref, cq_ref, ck_ref, o_ref, *, tq, ch, qpos0, n_keys):
    n_ch = ck_ref.shape[2]
    q0 = qpos0 + pl.program_id(2) * tq
    qpos = q0 + _iota((tq, 1), 0)
    c_hi = _causal_chunks(q0, tq, n_ch, ch)
    cq = cq_ref[0, 0]

    def mask_fn(c, half):
        kpos = c * ch + _iota((1, ch), 1)
        return (kpos <= qpos) & (kpos < n_keys)

    def bias_fn(c, half):
        return cq[:, half:half + 1] - ck_ref[0, 0, c, half:half + 1, :]

    o = _flash_pair(q_ref[0], k_ref, v_ref, 0, c_hi, ch, mask_fn, bias_fn, HEAD_DIM ** -0.5)
    o_ref[0] = o.astype(o_ref.dtype)


def _fox(q16, kv16, cq, ck, col, tq, qpos0, n_keys):
    b, t, _ = q16.shape
    s = kv16.shape[1]
    n_pairs = cq.shape[1]
    ch = ck.shape[4]
    return pl.pallas_call(
        functools.partial(_fox_body, tq=tq, ch=ch, qpos0=qpos0, n_keys=n_keys),
        grid=(b, n_pairs, t // tq),
        in_specs=[
            pl.BlockSpec((1, tq, LANES), lambda bi, hp, i: (bi, i, col["q"] + hp)),
            pl.BlockSpec((1, s, LANES), lambda bi, hp, i: (bi, 0, col["k"] + hp)),
            pl.BlockSpec((1, s, LANES), lambda bi, hp, i: (bi, 0, col["v"] + hp)),
            pl.BlockSpec((1, 1, tq, 2), lambda bi, hp, i: (bi, hp, i, 0)),
            pl.BlockSpec((1, 1, s // ch, 2, ch), lambda bi, hp, i: (bi, hp, 0, 0, 0)),
        ],
        out_specs=pl.BlockSpec((1, tq, LANES), lambda bi, hp, i: (bi, i, hp)),
        out_shape=jax.ShapeDtypeStruct((b, t, n_pairs * LANES), BF16),
        compiler_params=pltpu.CompilerParams(dimension_semantics=("parallel", "parallel", "arbitrary"),
                                             vmem_limit_bytes=VMEM_LIMIT),
        name="fox",
    )(q16, kv16, kv16, cq, ck)


E_QA, E_KA, E_VA, E_IQ, E_QB, E_IK, E_CK, E_CV, E_SK, E_SV, E_WK, E_WV, E_MISC = 0, 4, 5, 6, 8, 12, 13, 14, 15, 16, 17, 18, 19
E_CHUNKS = 20
G_QA, G_KA, G_QB, G_CMP, G_SEL, G_WIN = range(6)
E_KINDS = ((NORM_ROPE, G_QA),) * 4 + ((NORM_ROPE, G_KA), (PLAIN, 0)) + ((ROPE, 0),) * 2 + ((NORM_ROPE, G_QB),) * 4 + (
    (ROPE, 0), (NORM_ROPE, G_CMP), (PLAIN, 0), (NORM_ROPE, G_SEL), (PLAIN, 0), (NORM_ROPE, G_WIN), (PLAIN, 0), (PLAIN, 0))
O_Q, O_K, O_V, O_MISC = 0, 8, 16, 24
O_CHUNKS = 25
O_KINDS = ((NORM, 0),) * 8 + ((NORM, 1),) * 8 + ((PLAIN, 0),) * 8 + ((LOGSIG, 0),)


def _pair_heads(w, n_groups):
    d = w.shape[0]
    nj = w.shape[1] // (n_groups * HEAD_DIM)
    return w.reshape(d, n_groups, nj, HEAD_DIM).transpose(0, 2, 1, 3).reshape(d, -1)


def _even_weight(w_in):
    d = w_in.shape[0]
    sizes = (512, 128, 128, 256, 64, 4, 512, 128, 128, 128, 128, 128, 128, 24)
    offs = np.cumsum((0,) + sizes)
    qa, ka, va, iq, ik, iw, qb, ck, cv, sk, sv, wk, wv, gl = (w_in[:, offs[i]:offs[i + 1]] for i in range(14))
    misc = jnp.concatenate([iw, gl, jnp.zeros((d, LANES - 28), w_in.dtype)], axis=1)
    cols = [_pair_heads(qa, A_KV), ka, va, iq, _pair_heads(qb, B_KV), ik, ik, ck, cv, sk, sv, wk, wv, misc]
    return jnp.concatenate(cols, axis=1).astype(BF16)


def _odd_weight(w_in):
    d = w_in.shape[0]
    n = O_MISC * LANES
    fl = w_in[:, n:]
    return jnp.concatenate([w_in[:, :n], fl, jnp.zeros((d, LANES - fl.shape[1]), w_in.dtype)], axis=1).astype(BF16)


def _gain_table(gains):
    rows = [jnp.tile(g, 2) for g in gains]
    rows += [jnp.zeros((LANES,), F32)] * (8 - len(rows))
    return jnp.stack(rows)


def _rope_tables(pos):
    half = HEAD_DIM // 2
    inv = ROPE_THETA ** (-jnp.arange(half, dtype=F32) / half)
    ang = pos.astype(F32)[:, None] * inv[None, :]
    cos, sin = jnp.cos(ang), jnp.sin(ang)
    return jnp.tile(jnp.concatenate([cos, cos], axis=1), (1, 2)), jnp.tile(jnp.concatenate([-sin, sin], axis=1), (1, 2))


def _pair_rows(w_out_half, n_groups):
    nj = w_out_half.shape[0] // (n_groups * HEAD_DIM)
    return w_out_half.reshape(n_groups, nj, HEAD_DIM, -1).transpose(1, 0, 2, 3).reshape(w_out_half.shape)


def _compress_weight(w):
    w3 = w.reshape(CMP_LEN, HEAD_DIM, HEAD_DIM)
    eye = jnp.eye(B_KV, dtype=w.dtype)
    w5 = w3[:, None, :, None, :] * eye[None, :, None, :, None]
    return w5.reshape(CMP_LEN * B_KV * HEAD_DIM, B_KV * HEAD_DIM).astype(BF16)


def _importance_matrix(nc, ncp, ns, nsp):
    r, m = SEL_LEN // CMP_STRIDE, CMP_LEN // CMP_STRIDE
    mat = np.zeros((ncp, nsp), np.float32)
    for n in range(ns):
        for a in range(r):
            for b_ in range(m):
                c = n * r - m + 1 + a + b_
                if 0 <= c < nc:
                    mat[c, n] += 1.0
    return jnp.asarray(mat)


def _expand_matrix(nsp, s, ch):
    blk = np.arange(s) // SEL_LEN
    mat = (np.arange(nsp)[:, None] == blk[None, :]).astype(np.float32)
    return jnp.asarray(mat.reshape(nsp, s // ch, ch).transpose(1, 0, 2), dtype=BF16)


def _chunk(a, c, n=1):
    return a[..., c * LANES:(c + n) * LANES]


def _pad_rows(a, rows):
    return jnp.pad(a, ((0, 0), (0, rows - a.shape[1]), (0, 0)))


def kernel(x_prompt, x_sample, cache_a_k, cache_a_v, cache_a_idx, cache_b_cmp_k, cache_b_cmp_v, cache_b_sel_k, cache_b_sel_v, state_b_win_k, state_b_win_v, cache_c_k, cache_c_v, cache_c_logf, page_table, norm_mix, norm_ffn, e_w_in, e_qn_a, e_kn_a, e_qn_b, e_kn_cmp, e_kn_sel, e_kn_win, e_cmp_pos, e_w_cmp_k, e_w_cmp_v, e_w_out, o_w_in, o_b_f, o_qn, o_kn, o_w_out, w_up, w_down):
    bp, t, d = x_prompt.shape
    bs, ts, _ = x_sample.shape
    depth = norm_mix.shape[0]
    n_pages = page_table.shape[1]
    past = n_pages * PAGE_SIZE
    n_keys_s = past + ts
    s_s = _round_up(n_keys_s, 2 * PAGE_SIZE)
    n_blocks_s = s_s // PAGE_SIZE
    tq_p = 128
    tq_s = 16
    n_buf = state_b_win_k.shape[2]
    c_heads = cache_c_k.shape[3]

    tm_p = 256
    xp = x_prompt.reshape(bp * t, d)
    xs = x_sample.reshape(bs * ts, d)
    cos_p, sin_p = _rope_tables(jnp.arange(t))
    cos_s, sin_s = (jnp.tile(a, (bs, 1)) for a in _rope_tables(past + jnp.arange(ts)))
    zero_bias = jnp.zeros((1, LANES), F32)

    def flat_cache(c):
        return c.reshape(c.shape[:3] + (-1,))

    def new_block(rows):
        return _pad_rows(rows, PAGE_SIZE)

    def nsa_consts(n_keys, s):
        nc = (n_keys - CMP_LEN) // CMP_STRIDE + 1
        ncp = _round_up(nc, LANES)
        ns = -(-n_keys // SEL_LEN)
        nsp = _round_up(ns, LANES)
        ch = 512 if s % 512 == 0 else 256
        return ncp, min(SEL_TOP, ns), _importance_matrix(nc, ncp, ns, nsp), _expand_matrix(nsp, s, ch)

    def fit_rows(a, rows):
        return a[:, :rows] if a.shape[1] >= rows else _pad_rows(a, rows)

    ev_p, ev_s, od_p, od_s = [], [], [], []
    for li in range(depth):
        j = li // 2
        if li % 2 == 0:
            w = _even_weight(e_w_in[j])
            gains = _gain_table([e_qn_a[j], e_kn_a[j], e_qn_b[j], e_kn_cmp[j], e_kn_sel[j], e_kn_win[j]])
            hp32, hp16 = _inproj(xp, norm_mix[li], w, gains, cos_p, sin_p, zero_bias, E_KINDS, tm_p)
            hs32, hs16 = _inproj(xs, norm_mix[li], w, gains, cos_s, sin_s, zero_bias, E_KINDS, bs * ts)
            hp32, hp16 = hp32.reshape(bp, t, -1), hp16.reshape(bp, t, -1)
            hs32, hs16 = hs32.reshape(bs, ts, -1), hs16.reshape(bs, ts, -1)
            wk2, wv2 = _compress_weight(e_w_cmp_k[j]), _compress_weight(e_w_cmp_v[j])
            pe2 = jnp.pad(jnp.tile(e_cmp_pos[j][:, None, :], (1, B_KV, 1)).reshape(1, -1), ((0, 15), (0, 0))).astype(BF16)
            cu = CMP_STRIDE * LANES

            st_p = [_chunk(hp32, c) for c in (E_KA, E_VA)]
            st_p.append(_chunk(hp32, E_IK)[..., :IDX_DIM])
            st_p += [_chunk(hp32, c) for c in (E_CK, E_CV, E_SK, E_SV)]
            n_keep = min(WINDOW, t)
            st_p += [_chunk(hp32, c)[:, t - n_keep:] for c in (E_WK, E_WV)]
            ncp, n_sel, imp_mat, exp_mat = nsa_consts(t, t)
            cmpk, cmpv = _compress(_chunk(hp16, E_CK).reshape(bp, t // CMP_STRIDE, cu),
                                   _chunk(hp16, E_CV).reshape(bp, t // CMP_STRIDE, cu), wk2, wv2, pe2)
            col = dict(qa=E_QA, iq=E_IQ, misc=E_MISC, k=E_KA, v=E_VA, ik=E_IK,
                       qb=E_QB, sk=E_SK, sv=E_SV, wk=E_WK, wv=E_WV)
            oa_p = _dsa(hp16, hp32, hp16, col, tq_p, min(DSA_TOPK, t // 4), 0, t)
            ob_p = _nsa(hp16, hp32, fit_rows(cmpk, ncp), fit_rows(cmpv, ncp), hp16, hp16, imp_mat, exp_mat,
                        col, tq_p, n_sel, 0, 0, t)

            st_s = [_chunk(hs32, c) for c in (E_KA, E_VA)]
            st_s.append(_chunk(hs32, E_IK)[..., :IDX_DIM])
            st_s += [_chunk(hs32, c) for c in (E_CK, E_CV, E_SK, E_SV)]
            wk_all = jnp.concatenate([state_b_win_k[j].reshape(bs, n_buf, -1), _chunk(hs32, E_WK)], axis=1)
            wv_all = jnp.concatenate([state_b_win_v[j].reshape(bs, n_buf, -1), _chunk(hs32, E_WV)], axis=1)
            n_keep = min(WINDOW, n_buf + ts)
            st_s += [wk_all[:, n_buf + ts - n_keep:], wv_all[:, n_buf + ts - n_keep:]]
            caches = [flat_cache(c) for c in (cache_a_k, cache_a_v, cache_a_idx, cache_b_cmp_k, cache_b_cmp_v,
                                              cache_b_sel_k, cache_b_sel_v)]
            news = [new_block(a) for a in st_s[:7]]
            gk, gv, gik, gck, gcv, gsk, gsv = _gather(
                page_table, caches, j, news, (False, False, True, False, False, False, False), (BF16,) * 7, n_blocks_s)
            kv_s = jnp.concatenate([gk, gv, gik, gsk, gsv], axis=-1)
            kcol = dict(k=0, v=1, ik=2, sk=3, sv=4, wk=0, wv=1)
            qs16, qs32 = _pad_rows(hs16, tq_s), _pad_rows(hs32, tq_s)
            col_s = dict(col, **kcol)
            ncp, n_sel, imp_mat, exp_mat = nsa_consts(n_keys_s, s_s)
            cmpk, cmpv = _compress(gck.reshape(bs, s_s // CMP_STRIDE, cu), gcv.reshape(bs, s_s // CMP_STRIDE, cu),
                                   wk2, wv2, pe2)
            sw = _round_up(n_buf + ts, LANES)
            w_s = jnp.concatenate([_pad_rows(wk_all, sw), _pad_rows(wv_all, sw)], axis=-1).astype(BF16)
            oa_s = _dsa(qs16, qs32, kv_s, col_s, tq_s, min(DSA_TOPK, n_keys_s // 4), past, n_keys_s)
            ob_s = _nsa(qs16, qs32, fit_rows(cmpk, ncp), fit_rows(cmpv, ncp), kv_s, w_s, imp_mat, exp_mat,
                        col_s, tq_s, n_sel, past, past - n_buf, n_buf + ts)
            ev_p.append(st_p)
            ev_s.append(st_s)
            half = e_w_out.shape[1] // 2
            w_outs = [_pair_rows(e_w_out[j][:half], A_KV).astype(BF16), _pair_rows(e_w_out[j][half:], B_KV).astype(BF16)]
            os_p = [oa_p.reshape(bp * t, -1), ob_p.reshape(bp * t, -1)]
            os_s = [oa_s[:, :ts].reshape(bs * ts, -1), ob_s[:, :ts].reshape(bs * ts, -1)]
        else:
            w = _odd_weight(o_w_in[j])
            gains = _gain_table([o_qn[j], o_kn[j]])
            bias = jnp.pad(o_b_f[j], (0, LANES - c_heads)).reshape(1, LANES)
            hp32, hp16 = _inproj(xp, norm_mix[li], w, gains, cos_p, sin_p, bias, O_KINDS, tm_p)
            hs32, hs16 = _inproj(xs, norm_mix[li], w, gains, cos_s, sin_s, bias, O_KINDS, bs * ts)
            hp32, hp16 = hp32.reshape(bp, t, -1), hp16.reshape(bp, t, -1)
            hs32, hs16 = hs32.reshape(bs, ts, -1), hs16.reshape(bs, ts, -1)
            n_pairs = c_heads // 2
            col = dict(q=O_Q, k=O_K, v=O_V)

            def fox_bias(cum, qlo, tq_pad, ch):
                b_, _, s_ = cum.shape
                cq = cum[:, :, qlo:qlo + tq_pad].reshape(b_, n_pairs, 2, tq_pad).transpose(0, 1, 3, 2)
                ck = cum.reshape(b_, n_pairs, 2, s_ // ch, ch).transpose(0, 1, 3, 2, 4)
                return cq, ck

            logf_p = _chunk(hp32, O_MISC)[..., :c_heads]
            st_p = [_chunk(hp32, O_K, 8), _chunk(hp32, O_V, 8), logf_p]
            cum_p = _cumsum(logf_p.transpose(0, 2, 1))
            cq, ck = fox_bias(cum_p, 0, t, 512)
            o_p = _fox(hp16, hp16, cq, ck, col, tq_p, 0, t)

            logf_s = _chunk(hs32, O_MISC)[..., :c_heads]
            st_s = [_chunk(hs32, O_K, 8), _chunk(hs32, O_V, 8), logf_s]
            gk, gv = _gather(page_table, [flat_cache(cache_c_k), flat_cache(cache_c_v)], j,
                             [new_block(st_s[0]), new_block(st_s[1])], (False, False), (BF16, BF16), n_blocks_s)
            (glogf,) = _gather(page_table, [cache_c_logf], j, [new_block(logf_s)], (False,), (F32,), n_blocks_s)
            cum_s = _cumsum(glogf.transpose(0, 2, 1))
            cq, ck = fox_bias(cum_s, past, tq_s, 256)
            kv_s = jnp.concatenate([gk, gv], axis=-1)
            o_s = _fox(_pad_rows(hs16, tq_s), kv_s, cq, ck, dict(q=O_Q, k=0, v=8), tq_s, past, n_keys_s)
            od_p.append(st_p)
            od_s.append(st_s)
            w_outs = [o_w_out[j].astype(BF16)]
            os_p = [o_p.reshape(bp * t, -1)]
            os_s = [o_s[:, :ts].reshape(bs * ts, -1)]
        wu, wd = w_up[li].astype(BF16), w_down[li].astype(BF16)
        xp = _postmix(xp, os_p, w_outs, norm_ffn[li], wu, wd, tm_p)
        xs = _postmix(xs, os_s, w_outs, norm_ffn[li], wu, wd, bs * ts)

    def stk(states, n, rows):
        return jnp.stack([s[n] for s in states]).reshape((len(states),) + rows)

    outs = [xp.reshape(bp, t, d), xs.reshape(bs, ts, d)]
    n_even = len(ev_p)
    for n in range(9):
        for states, b_, t_ in ((ev_p, bp, t), (ev_s, bs, ts)):
            rows = states[0][n].shape[1]
            tail = (IDX_DIM,) if n == 2 else (A_KV, HEAD_DIM)
            outs.append(jnp.stack([s[n] for s in states]).reshape((n_even, b_, rows) + tail))
    for n in range(3):
        for states, b_, t_ in ((od_p, bp, t), (od_s, bs, ts)):
            tail = (c_heads,) if n == 2 else (c_heads, HEAD_DIM)
            outs.append(jnp.stack([s[n] for s in states]).reshape((len(states), b_, t_) + tail))
    return tuple(outs)
```

```python
import functools

import numpy as np
import jax
import jax.numpy as jnp
from jax import lax
from jax.experimental import pallas as pl
from jax.experimental.pallas import tpu as pltpu

F32 = jnp.float32
BF16 = jnp.bfloat16

HEAD_DIM = 64
A_KV = 2
IDX_HEADS = 4
IDX_DIM = 64
DSA_TOPK = 256
B_KV = 2
CMP_LEN = 32
CMP_STRIDE = 16
SEL_LEN = 64
SEL_TOP = 16
WINDOW = 512
ROPE_THETA = 10000.0
RMS_EPS = 1e-6
NEG_BIG = -1e30
FORCE_SCORE = 1e9
PAGE_SIZE = 128

LANES = 128
BF16_SUBLANES = 16
VMEM_LIMIT = 56 * 1024 * 1024

GATHER_PAGES = 4
KEY_CHUNK = GATHER_PAGES * PAGE_SIZE
WIN_CHUNK = 256
ATT_UNROLL = 2
TQ_PROMPT = 256
TQ_SAMPLE = BF16_SUBLANES
TM_ROWS = 256
FF_CHUNK = 1024
LOGF_PAGES = 8

PLAIN, ROPE, NORM_ROPE, NORM, LOGSIG = range(5)

INT_MIN = np.int32(-2 ** 31)
KEY_NINF = np.int32(-2139095041)


def _dot_nt(a, b):
    return lax.dot_general(a, b, (((1,), (1,)), ((), ())), preferred_element_type=F32)


def _iota(shape, dim):
    return lax.broadcasted_iota(jnp.int32, shape, dim)


def _round_up(x, m):
    return -(-x // m) * m


def _div_pow2(x, n):
    assert n & (n - 1) == 0
    if isinstance(x, int):
        return x // n
    return lax.shift_right_arithmetic(x, jnp.int32(n.bit_length() - 1))


def _loop(lo, hi, body, init, static, unroll=1):
    if static:
        carry = init
        for c in range(lo, hi):
            carry = body(c, carry)
        return carry
    if unroll == 1:
        return lax.fori_loop(lo, hi, body, init)

    def trip(i, carry):
        for u in range(unroll):
            carry = body(i * unroll + u, carry)
        return carry

    return lax.fori_loop(_div_pow2(lo, unroll), _div_pow2(hi, unroll), trip, init)


def _round_up_pow2(x, n):
    return _div_pow2(x + (n - 1), n) * n


def _min(a, b):
    return min(a, b) if isinstance(a, int) and isinstance(b, int) else jnp.minimum(a, b)


def _max(a, b):
    return max(a, b) if isinstance(a, int) and isinstance(b, int) else jnp.maximum(a, b)


def _chunk_start(c, ch):
    return c * ch if isinstance(c, int) else pl.multiple_of(c * ch, ch)


def _inproj_body(x_ref, g_ref, w_ref, gains_ref, cos_ref, sin_ref, bias_ref, o32_ref, o16_ref, *, kinds):
    x = x_ref[...]
    xn = x * lax.rsqrt(jnp.mean(x * x, axis=-1, keepdims=True) + RMS_EPS) * g_ref[...]
    xb = xn.astype(BF16)
    cos = cos_ref[...]
    sin = sin_ref[...]
    lane = _iota((1, LANES), 1)
    lo = lane < HEAD_DIM
    first_half = (lane & (HEAD_DIM - 1)) < (HEAD_DIM // 2)
    n_chunks = len(kinds)
    group = 4
    for c0 in range(0, n_chunks, group):
        c1 = min(c0 + group, n_chunks)
        hh = jnp.dot(xb, w_ref[:, c0 * LANES:c1 * LANES], preferred_element_type=F32)
        for c in range(c0, c1):
            h = hh[:, (c - c0) * LANES:(c - c0 + 1) * LANES]
            kind, gi = kinds[c]
            if kind in (NORM_ROPE, NORM):
                ss = h * h
                s_lo = jnp.sum(jnp.where(lo, ss, 0.0), axis=-1, keepdims=True)
                s_hi = jnp.sum(jnp.where(lo, 0.0, ss), axis=-1, keepdims=True)
                ms = jnp.where(lo, s_lo, s_hi) * (1.0 / HEAD_DIM)
                h = h * lax.rsqrt(ms + RMS_EPS) * gains_ref[gi:gi + 1, :]
            if kind in (NORM_ROPE, ROPE):
                partner = jnp.where(first_half, pltpu.roll(h, LANES - HEAD_DIM // 2, 1),
                                    pltpu.roll(h, HEAD_DIM // 2, 1))
                h = h * cos + partner * sin
            if kind == LOGSIG:
                z = h + bias_ref[...]
                h = -(jnp.maximum(-z, 0.0) + jnp.log1p(jnp.exp(-jnp.abs(z))))
            o32_ref[:, c * LANES:(c + 1) * LANES] = h
            o16_ref[:, c * LANES:(c + 1) * LANES] = h.astype(BF16)


def _inproj(x, g, w, gains, cos, sin, bias, kinds, tm):
    n, d = x.shape
    nout = w.shape[1]
    n_tab = cos.shape[0] // tm
    const = lambda i: (0, 0)
    return pl.pallas_call(
        functools.partial(_inproj_body, kinds=kinds),
        grid=(n // tm,),
        in_specs=[
            pl.BlockSpec((tm, d), lambda i: (i, 0)),
            pl.BlockSpec((1, d), const),
            pl.BlockSpec((d, nout), const, pipeline_mode=pl.Buffered(1)),
            pl.BlockSpec(gains.shape, const),
            pl.BlockSpec((tm, LANES), lambda i: (i % n_tab, 0)),
            pl.BlockSpec((tm, LANES), lambda i: (i % n_tab, 0)),
            pl.BlockSpec((1, LANES), const),
        ],
        out_specs=[pl.BlockSpec((tm, nout), lambda i: (i, 0)),
                   pl.BlockSpec((tm, nout), lambda i: (i, 0))],
        out_shape=[jax.ShapeDtypeStruct((n, nout), F32), jax.ShapeDtypeStruct((n, nout), BF16)],
        compiler_params=pltpu.CompilerParams(dimension_semantics=("parallel",), vmem_limit_bytes=VMEM_LIMIT),
        name="inproj",
    )(x, g.reshape(1, d), w, gains, cos, sin, bias)


def _postmix_body(*refs, n_o):
    x_ref = refs[0]
    o_refs = refs[1:1 + n_o]
    w_ref, g_ref, wup_ref, wdn_ref, out_ref = refs[1 + n_o:]
    o = o_refs[0][...] if n_o == 1 else jnp.concatenate([r[...] for r in o_refs], axis=-1)
    x1 = x_ref[...] + jnp.dot(o, w_ref[...], preferred_element_type=F32)
    xn = x1 * lax.rsqrt(jnp.mean(x1 * x1, axis=-1, keepdims=True) + RMS_EPS) * g_ref[...]
    xb = xn.astype(BF16)
    acc = x1
    d_ff = wup_ref.shape[1]
    for c in range(d_ff // FF_CHUNK):
        h = jnp.dot(xb, wup_ref[:, c * FF_CHUNK:(c + 1) * FF_CHUNK], preferred_element_type=F32)
        a = jnp.square(jnp.maximum(h, 0.0)).astype(BF16)
        acc = acc + jnp.dot(a, wdn_ref[c * FF_CHUNK:(c + 1) * FF_CHUNK, :], preferred_element_type=F32)
    out_ref[...] = acc


def _postmix(x, os_, ws, g, w_up, w_down, tm):
    n, d = x.shape
    const = lambda i: (0, 0)
    row = lambda i: (i, 0)
    w_out = ws[0] if len(ws) == 1 else jnp.concatenate(ws, axis=0)
    in_specs = [pl.BlockSpec((tm, d), row)]
    in_specs += [pl.BlockSpec((tm, o.shape[1]), row) for o in os_]
    in_specs += [pl.BlockSpec(w_out.shape, const, pipeline_mode=pl.Buffered(1)),
                 pl.BlockSpec((1, d), const),
                 pl.BlockSpec(w_up.shape, const, pipeline_mode=pl.Buffered(1)),
                 pl.BlockSpec(w_down.shape, const, pipeline_mode=pl.Buffered(1))]
    return pl.pallas_call(
        functools.partial(_postmix_body, n_o=len(os_)),
        grid=(n // tm,),
        in_specs=in_specs,
        out_specs=pl.BlockSpec((tm, d), row),
        out_shape=jax.ShapeDtypeStruct((n, d), F32),
        compiler_params=pltpu.CompilerParams(dimension_semantics=("parallel",), vmem_limit_bytes=VMEM_LIMIT),
        name="postmix",
    )(x, *os_, w_out, g.reshape(1, d), w_up, w_down)


def _gather_body(pt_ref, *refs, n, pps, n_steps, dup, has_new):
    del pt_ref
    page_refs = refs[:n * pps]
    new_refs = refs[n * pps:n * pps + (n if has_new else 0)]
    out_refs = refs[n * pps + len(new_refs):]
    rows = page_refs[0].shape[0]

    def widen(val, a):
        return jnp.concatenate([val, val], axis=-1) if dup[a] else val

    def copy_pages():
        for a in range(n):
            for u in range(pps):
                out_refs[a][u * rows:(u + 1) * rows, :] = widen(page_refs[a * pps + u][...], a).astype(out_refs[a].dtype)

    if not has_new:
        copy_pages()
        return
    p = pl.program_id(1)
    pl.when(p < n_steps)(copy_pages)

    @pl.when(p == n_steps)
    def _():
        for a in range(n):
            new = widen(new_refs[a][...], a)
            r = new.shape[0]
            out_refs[a][0:r, :] = new.astype(out_refs[a].dtype)
            out_refs[a][r:, :] = jnp.zeros((pps * rows - r, new.shape[1]), out_refs[a].dtype)


def _gather(page_table, caches, layer, news, dup, out_dtypes, pps):
    bs, n_pages = page_table.shape
    assert n_pages % pps == 0
    n_steps = n_pages // pps
    n = len(caches)
    has_new = news is not None
    rows = caches[0].shape[2]
    in_specs, out_specs, out_shape = [], [], []
    for c in caches:
        for u in range(pps):
            in_specs.append(pl.BlockSpec(
                (None, None, rows, c.shape[-1]),
                lambda b, p, pt, u=u: (layer, pt[b, jnp.minimum(p, n_steps - 1) * pps + u], 0, 0)))
    if has_new:
        for nw in news:
            in_specs.append(pl.BlockSpec((None,) + nw.shape[1:], lambda b, p, pt: (b, 0, 0)))
    for c, d_, dt in zip(caches, dup, out_dtypes):
        width = c.shape[-1] * (2 if d_ else 1)
        out_specs.append(pl.BlockSpec((None, pps * rows, width), lambda b, p, pt: (b, p, 0)))
        out_shape.append(jax.ShapeDtypeStruct((bs, (n_steps + has_new) * pps * rows, width), dt))
    args = [a for c in caches for a in [c] * pps] + (list(news) if has_new else [])
    return pl.pallas_call(
        functools.partial(_gather_body, n=n, pps=pps, n_steps=n_steps, dup=tuple(dup), has_new=has_new),
        grid_spec=pltpu.PrefetchScalarGridSpec(
            num_scalar_prefetch=1, grid=(bs, n_steps + has_new), in_specs=in_specs, out_specs=out_specs),
        out_shape=out_shape,
        compiler_params=pltpu.CompilerParams(dimension_semantics=("parallel", "arbitrary"),
                                             vmem_limit_bytes=VMEM_LIMIT),
        name="paged_gather",
    )(page_table, *args)


def _float_key(s):
    b = lax.bitcast_convert_type(s + 0.0, jnp.int32)
    return jnp.where(b >= 0, b, b ^ jnp.int32(0x7FFFFFFF))


def _prefix_tri():
    return jnp.where(_iota((LANES, LANES), 0) <= _iota((LANES, LANES), 1), 1.0, 0.0).astype(BF16)


def _topk_mask(key_ref, sel_ref, k, tri, n_used, static, additive):
    _, rows, width = key_ref.shape
    pieces = range(width // LANES)
    on, off = (0.0, NEG_BIG) if additive else (1.0, 0.0)

    def count(pred):
        def body(c, acc):
            for u in pieces:
                acc = acc + jnp.where(pred(key_ref[c, :, u * LANES:(u + 1) * LANES]), 1.0, 0.0)
            return acc
        return jnp.sum(_loop(0, n_used, body, jnp.zeros((rows, LANES), F32), static), axis=-1, keepdims=True)

    kf = float(k)
    zero = jnp.zeros((rows, 1), jnp.int32)
    t = jnp.where(count(lambda kc: kc >= zero) >= kf, zero, jnp.full((rows, 1), INT_MIN, jnp.int32))

    def bit_step(i, t):
        cand = t | jnp.left_shift(jnp.int32(1), 30 - i)
        return jnp.where(count(lambda kc: kc >= cand) >= kf, cand, t)

    t = _loop(0, 31, bit_step, t, not static)
    need = kf - count(lambda kc: kc > t)

    def mark(c, carry):
        for u in pieces:
            kc = key_ref[c, :, u * LANES:(u + 1) * LANES]
            eq = kc == t
            pc = jnp.dot(jnp.where(eq, 1.0, 0.0).astype(BF16), tri, preferred_element_type=F32) + carry
            sel = ((kc > t) | (eq & (pc <= need))) & (kc > KEY_NINF)
            sel_ref[c, :, u * LANES:(u + 1) * LANES] = jnp.where(sel, on, off)
            carry = jnp.broadcast_to(pc[:, LANES - 1:LANES], (rows, LANES))
        return carry

    _loop(0, n_used, mark, jnp.zeros((rows, LANES), F32), static)


def _attend_pair(q2, k_ref, v_ref, s_ref, bounds, ch, mask_fn, static, unroll=1, kidx=None):
    c_lo, c_mid, c_hi = bounds
    tq = q2.shape[0]
    pieces = range(ch // LANES)
    lo = _iota((1, LANES), 1) < HEAD_DIM
    zero = jnp.zeros_like(q2)
    qs = (jnp.where(lo, q2, zero), jnp.where(lo, zero, q2))
    rows_of = lambda c: pl.ds(_chunk_start(c if kidx is None else kidx(c), ch), ch)

    def score(c, mvecs, diag):
        kc = k_ref[0, rows_of(c), :]
        out = []
        for half in range(2):
            s = _dot_nt(qs[half], kc)
            extra = mask_fn(c, half, diag)
            if extra is not None:
                s = s + extra
            s_ref[half, c] = s
            mvec = mvecs[half]
            for u in pieces:
                mvec = jnp.maximum(mvec, s[:, u * LANES:(u + 1) * LANES])
            out.append(mvec)
        return tuple(out)

    mvecs = (jnp.full((tq, LANES), NEG_BIG, F32),) * 2
    mvecs = _loop(c_lo, c_mid, functools.partial(score, diag=False), mvecs, static, unroll)
    mvecs = _loop(c_mid, c_hi, functools.partial(score, diag=True), mvecs, static, unroll)
    ms = [jnp.max(mv, axis=-1, keepdims=True) for mv in mvecs]

    def accumulate(c, carry):
        vc = v_ref[0, rows_of(c), :]
        out = []
        for half in range(2):
            lvec, acc = carry[2 * half:2 * half + 2]
            p = jnp.exp(s_ref[half, c] - ms[half])
            for u in pieces:
                lvec = lvec + p[:, u * LANES:(u + 1) * LANES]
            out += [lvec, acc + jnp.dot(p.astype(BF16), vc, preferred_element_type=F32)]
        return tuple(out)

    res = _loop(c_lo, c_hi, accumulate, (jnp.zeros((tq, LANES), F32),) * 4, static, unroll)
    outs = []
    for half in range(2):
        l = jnp.sum(res[2 * half], axis=-1, keepdims=True)
        outs.append(jnp.where(ms[half] > 0.5 * NEG_BIG, res[2 * half + 1] / jnp.maximum(l, 1e-30), 0.0))
    return jnp.where(lo, outs[0], outs[1])


def _scaled(q, scale):
    return (q.astype(F32) * scale).astype(BF16)


def _query_block(qpos0, tq, axis, static):
    q0 = qpos0 if static else qpos0 + pl.program_id(axis) * tq
    return q0, q0 + _iota((tq, 1), 0)


def _causal_chunks(q0, tq, n_ch, ch):
    c_hi = _min(n_ch, _div_pow2(q0 + tq - 1, ch) + 1)
    c_mid = _min(c_hi, _div_pow2(q0 + 1, ch))
    if isinstance(c_hi, int):
        return c_mid, c_hi
    assert n_ch % ATT_UNROLL == 0
    return _div_pow2(c_mid, ATT_UNROLL) * ATT_UNROLL, _round_up_pow2(c_hi, ATT_UNROLL)


def _dsa_body(qa_ref, iq_ref, misc_ref, k_ref, v_ref, ik_ref, o_ref, key_ref, sel_ref, s_ref,
              *, tq, ch, topk, qpos0, n_keys, static):
    n_ch = key_ref.shape[0]
    q0, qpos = _query_block(qpos0, tq, 1, static)
    _, c_hi = _causal_chunks(q0, tq, n_ch, ch)
    lo = _iota((1, LANES), 1) < HEAD_DIM
    iw = misc_ref[0][:, 0:IDX_HEADS]
    tri = _prefix_tri()

    def score_body(c, _):
        ikc = ik_ref[0, pl.ds(_chunk_start(c, ch), ch), :]
        acc = jnp.zeros((tq, ch), F32)
        for hp in range(IDX_HEADS // 2):
            iqc = iq_ref[0, :, hp * LANES:(hp + 1) * LANES]
            zero = jnp.zeros_like(iqc)
            for half in range(2):
                qm = jnp.where(lo, iqc, zero) if half == 0 else jnp.where(lo, zero, iqc)
                s = _dot_nt(qm, ikc) * (IDX_DIM ** -0.5)
                h = 2 * hp + half
                acc = acc + jnp.maximum(s, 0.0) * iw[:, h:h + 1]
        sc = acc * (IDX_HEADS ** -0.5)
        kpos = c * ch + _iota((1, ch), 1)
        key_ref[c] = jnp.where((kpos <= qpos) & (kpos < n_keys), _float_key(sc), KEY_NINF)
        return 0

    _loop(0, c_hi, score_body, 0, static, ATT_UNROLL)
    _topk_mask(key_ref, sel_ref, topk, tri, c_hi, static, additive=True)

    for j in range(qa_ref.shape[2] // LANES):
        q2 = _scaled(qa_ref[0, :, j * LANES:(j + 1) * LANES], HEAD_DIM ** -0.5)
        o = _attend_pair(q2, k_ref, v_ref, s_ref, (0, 0, c_hi), ch, lambda c, half, diag: sel_ref[c],
                         static, ATT_UNROLL)
        o_ref[0, :, j * LANES:(j + 1) * LANES] = o.astype(o_ref.dtype)


def _dsa(q16, m32, src, col, tq, topk, qpos0, n_keys, static):
    b, t, _ = q16.shape
    s = src["k"][0].shape[1]
    ch = KEY_CHUNK
    n_qa = 4 * LANES
    kv_spec = lambda name: pl.BlockSpec((1, s, LANES), lambda bi, i: (bi, 0, src[name][1]))
    return pl.pallas_call(
        functools.partial(_dsa_body, tq=tq, ch=ch, topk=topk, qpos0=qpos0, n_keys=n_keys, static=static),
        grid=(b, t // tq),
        in_specs=[
            pl.BlockSpec((1, tq, n_qa), lambda bi, i: (bi, i, col["qa"] * LANES // n_qa)),
            pl.BlockSpec((1, tq, 2 * LANES), lambda bi, i: (bi, i, col["iq"] // 2)),
            pl.BlockSpec((1, tq, LANES), lambda bi, i: (bi, i, col["misc"])),
            kv_spec("k"), kv_spec("v"), kv_spec("ik"),
        ],
        out_specs=pl.BlockSpec((1, tq, n_qa), lambda bi, i: (bi, i, 0)),
        out_shape=jax.ShapeDtypeStruct((b, t, n_qa), BF16),
        scratch_shapes=[pltpu.VMEM((s // ch, tq, ch), jnp.int32), pltpu.VMEM((s // ch, tq, ch), F32),
                        pltpu.VMEM((2, s // ch, tq, ch), F32)],
        compiler_params=pltpu.CompilerParams(dimension_semantics=("parallel", "arbitrary"),
                                             vmem_limit_bytes=VMEM_LIMIT),
        name="dsa",
    )(q16, q16, m32, src["k"][0], src["v"][0], src["ik"][0])


def _nsa_body(qb_ref, misc_ref, cmpk_ref, cmpv_ref, sk_ref, sv_ref, wk_ref, wv_ref, imp_ref, exp_ref,
              o_ref, key_ref, selb_ref, msk_ref, s_ref, sw_ref,
              *, tq, ch, n_sel, qpos0, wpos0, n_wvalid, static):
    n_ch = exp_ref.shape[0]
    ncp = cmpk_ref.shape[1]
    nsp = imp_ref.shape[1]
    q0, qpos = _query_block(qpos0, tq, 1, static)
    _, c_hi = _causal_chunks(q0, tq, n_ch, ch)
    lo = _iota((1, LANES), 1) < HEAD_DIM
    scale = HEAD_DIM ** -0.5
    tri = _prefix_tri()
    n_chunks_q = qb_ref.shape[2] // LANES
    gates = jax.nn.sigmoid(misc_ref[0])

    cvalid = (_iota((1, ncp), 1) * CMP_STRIDE + (CMP_LEN - 1)) <= qpos
    ck = cmpk_ref[0]
    cv = cmpv_ref[0]
    imp_c = [jnp.zeros((tq, ncp), F32), jnp.zeros((tq, ncp), F32)]
    o_cmp = []
    for j in range(n_chunks_q):
        q2 = qb_ref[0, :, j * LANES:(j + 1) * LANES]
        zero = jnp.zeros_like(q2)
        outs = []
        for half in range(2):
            qm = jnp.where(lo, q2, zero) if half == 0 else jnp.where(lo, zero, q2)
            s = jnp.where(cvalid, _dot_nt(qm, ck) * scale, NEG_BIG)
            p = jnp.where(cvalid, jnp.exp(s - jnp.max(s, axis=-1, keepdims=True)), 0.0)
            pn = p / jnp.maximum(jnp.sum(p, axis=-1, keepdims=True), 1e-30)
            imp_c[half] = imp_c[half] + pn
            outs.append(jnp.dot(pn.astype(BF16), cv, preferred_element_type=F32))
        o_cmp.append(jnp.where(lo, outs[0], outs[1]))

    blk = _iota((1, nsp), 1)
    adm = blk * SEL_LEN <= qpos
    cur = _div_pow2(qpos, SEL_LEN)
    forced = adm & ((blk == 0) | (blk == cur) | (blk == cur - 1))
    for g in range(2):
        imp = jnp.dot(imp_c[g], imp_ref[...], preferred_element_type=F32, precision=lax.Precision.HIGHEST)
        score = jnp.where(forced, FORCE_SCORE, jnp.where(adm, imp, -jnp.inf))
        key_ref[0] = _float_key(score)
        _topk_mask(key_ref, selb_ref, n_sel, tri, 1, True, additive=False)
        bm = selb_ref[0].astype(BF16)

        def expand(c, _):
            kpos = c * ch + _iota((1, ch), 1)
            hit = (jnp.dot(bm, exp_ref[c], preferred_element_type=F32) > 0.5) & (kpos <= qpos)
            msk_ref[g, c] = jnp.where(hit, 0.0, NEG_BIG)
            return 0

        _loop(0, c_hi, expand, 0, static, ATT_UNROLL)

    wch = WIN_CHUNK
    n_wch = wk_ref.shape[1] // wch
    n_win = sw_ref.shape[1]
    wc0 = _div_pow2(q0 - WINDOW - wpos0, wch)

    def win_chunk(u):
        c = wc0 + u
        return _min(_max(c, 0), n_wch - 1)

    def win_mask(u, half, diag):
        row = (wc0 + u) * wch + _iota((1, wch), 1)
        kpos = wpos0 + row
        dlt = qpos - kpos
        ok = (dlt >= 0) & (dlt <= WINDOW) & (kpos >= 0) & (row >= 0) & (row < n_wvalid)
        return jnp.where(ok, 0.0, NEG_BIG)

    for j in range(n_chunks_q):
        q2 = _scaled(qb_ref[0, :, j * LANES:(j + 1) * LANES], scale)
        o_sel = _attend_pair(q2, sk_ref, sv_ref, s_ref, (0, 0, c_hi), ch, lambda c, half, diag: msk_ref[half, c],
                             static, ATT_UNROLL)
        o_win = _attend_pair(q2, wk_ref, wv_ref, sw_ref, (0, 0, n_win), wch, win_mask, True, kidx=win_chunk)
        o = jnp.zeros((tq, LANES), F32)
        for bi, ob in enumerate((o_cmp[j], o_sel, o_win)):
            c_a = IDX_HEADS + 3 * j + bi
            c_b = IDX_HEADS + 3 * (n_chunks_q + j) + bi
            o = o + jnp.where(lo, gates[:, c_a:c_a + 1], gates[:, c_b:c_b + 1]) * ob
        o_ref[0, :, j * LANES:(j + 1) * LANES] = o.astype(o_ref.dtype)


def _nsa(q16, m32, cmpk, cmpv, src, imp_mat, exp_mat, col, tq, n_sel, qpos0, wpos0, n_wvalid, static):
    b, t, _ = q16.shape
    s = src["sk"][0].shape[1]
    sw = src["wk"][0].shape[1]
    ch = exp_mat.shape[2]
    ncp, nsp = imp_mat.shape
    n_qb = 4 * LANES
    assert t == tq or tq % WIN_CHUNK == 0
    mis = (qpos0 - WINDOW - wpos0) % WIN_CHUNK
    n_win = (mis + WINDOW + tq - 1) // WIN_CHUNK + 1
    spec = lambda name, rows: pl.BlockSpec((1, rows, LANES), lambda bi, i: (bi, 0, src[name][1]))
    return pl.pallas_call(
        functools.partial(_nsa_body, tq=tq, ch=ch, n_sel=n_sel, qpos0=qpos0, wpos0=wpos0, n_wvalid=n_wvalid,
                          static=static),
        grid=(b, t // tq),
        in_specs=[
            pl.BlockSpec((1, tq, n_qb), lambda bi, i: (bi, i, col["qb"] * LANES // n_qb)),
            pl.BlockSpec((1, tq, LANES), lambda bi, i: (bi, i, col["misc"])),
            pl.BlockSpec((1, ncp, LANES), lambda bi, i: (bi, 0, 0)),
            pl.BlockSpec((1, ncp, LANES), lambda bi, i: (bi, 0, 0)),
            spec("sk", s), spec("sv", s), spec("wk", sw), spec("wv", sw),
            pl.BlockSpec((ncp, nsp), lambda bi, i: (0, 0)),
            pl.BlockSpec(exp_mat.shape, lambda bi, i: (0, 0, 0)),
        ],
        out_specs=pl.BlockSpec((1, tq, n_qb), lambda bi, i: (bi, i, 0)),
        out_shape=jax.ShapeDtypeStruct((b, t, n_qb), BF16),
        scratch_shapes=[pltpu.VMEM((1, tq, nsp), jnp.int32), pltpu.VMEM((1, tq, nsp), F32),
                        pltpu.VMEM((2, s // ch, tq, ch), F32), pltpu.VMEM((2, s // ch, tq, ch), F32),
                        pltpu.VMEM((2, n_win, tq, WIN_CHUNK), F32)],
        compiler_params=pltpu.CompilerParams(dimension_semantics=("parallel", "arbitrary"),
                                             vmem_limit_bytes=VMEM_LIMIT),
        name="nsa",
    )(q16, m32, cmpk, cmpv, src["sk"][0], src["sv"][0], src["wk"][0], src["wv"][0], imp_mat, exp_mat)


def _compress_body(uk_ref, uv_ref, wk_ref, wv_ref, pe_ref, ok_ref, ov_ref, shift_ref):
    nu = uk_ref.shape[1]
    half = wk_ref.shape[0] // 2
    shift_ref[nu:nu + 8, :] = jnp.zeros((8, LANES), F32)
    for u_ref, w_ref, o_ref in ((uk_ref, wk_ref, ok_ref), (uv_ref, wv_ref, ov_ref)):
        u = u_ref[0]
        top = jnp.dot(u, w_ref[0:half, :], preferred_element_type=F32)
        shift_ref[0:nu, :] = jnp.dot(u, w_ref[half:2 * half, :], preferred_element_type=F32)
        pec = jnp.dot(pe_ref[...], w_ref[...], preferred_element_type=F32)[0:1, :]
        o_ref[0] = (top + shift_ref[1:nu + 1, :] + pec).astype(o_ref.dtype)


def _compress(uk, uv, wk2, wv2, pe2):
    b, nu, width = uk.shape
    const = lambda bi: (0, 0)
    row = lambda bi: (bi, 0, 0)
    return pl.pallas_call(
        _compress_body,
        grid=(b,),
        in_specs=[pl.BlockSpec((1, nu, width), row), pl.BlockSpec((1, nu, width), row),
                  pl.BlockSpec(wk2.shape, const), pl.BlockSpec(wv2.shape, const), pl.BlockSpec(pe2.shape, const)],
        out_specs=[pl.BlockSpec((1, nu, LANES), row), pl.BlockSpec((1, nu, LANES), row)],
        out_shape=[jax.ShapeDtypeStruct((b, nu, LANES), BF16)] * 2,
        scratch_shapes=[pltpu.VMEM((nu + 8, LANES), F32)],
        compiler_params=pltpu.CompilerParams(dimension_semantics=("parallel",), vmem_limit_bytes=VMEM_LIMIT),
        name="compress",
    )(uk, uv, wk2, wv2, pe2)


def _cumsum_body(x_ref, o_ref):
    s = x_ref.shape[2]
    tri = jnp.where(_iota((LANES, LANES), 0) <= _iota((LANES, LANES), 1), 1.0, 0.0)
    carry = jnp.zeros((x_ref.shape[1], 1), F32)
    for c in range(s // LANES):
        blk = x_ref[0, :, c * LANES:(c + 1) * LANES]
        pc = jnp.dot(blk, tri, preferred_element_type=F32, precision=lax.Precision.HIGHEST) + carry
        o_ref[0, :, c * LANES:(c + 1) * LANES] = pc
        carry = pc[:, LANES - 1:LANES]


def _cumsum(x):
    b, h, s = x.shape
    return pl.pallas_call(
        _cumsum_body,
        grid=(b,),
        in_specs=[pl.BlockSpec((1, h, s), lambda bi: (bi, 0, 0))],
        out_specs=pl.BlockSpec((1, h, s), lambda bi: (bi, 0, 0)),
        out_shape=jax.ShapeDtypeStruct((b, h, s), F32),
        compiler_params=pltpu.CompilerParams(dimension_semantics=("parallel",)),
        name="cumsum",
    )(x)


def _fox_body(q_ref, k_ref, v_ref, cq_ref, ck_ref, o_ref, s_ref, *, tq, ch, qpos0, n_keys, static):
    n_ch = ck_ref.shape[2]
    q0, qpos = _query_block(qpos0, tq, 2, static)
    c_mid, c_hi = _causal_chunks(q0, tq, n_ch, ch)
    cq = cq_ref[0, 0]

    def bias_fn(c, half, diag):
        bias = cq[:, half:half + 1] - ck_ref[0, 0, c, half:half + 1, :]
        if diag:
            kpos = c * ch + _iota((1, ch), 1)
            bias = jnp.where((kpos <= qpos) & (kpos < n_keys), bias, NEG_BIG)
        return bias

    q2 = _scaled(q_ref[0], HEAD_DIM ** -0.5)
    o = _attend_pair(q2, k_ref, v_ref, s_ref, (0, c_mid, c_hi), ch, bias_fn, static, ATT_UNROLL)
    o_ref[0] = o.astype(o_ref.dtype)


def _fox(q16, src, cq, ck, qcol, tq, qpos0, n_keys, static):
    b, t, _ = q16.shape
    s = src["k"][0].shape[1]
    n_pairs = cq.shape[1]
    ch = ck.shape[4]
    kv_spec = lambda name: pl.BlockSpec((1, s, LANES), lambda bi, hp, i: (bi, 0, src[name][1] + hp))
    return pl.pallas_call(
        functools.partial(_fox_body, tq=tq, ch=ch, qpos0=qpos0, n_keys=n_keys, static=static),
        grid=(b, n_pairs, t // tq),
        in_specs=[
            pl.BlockSpec((1, tq, LANES), lambda bi, hp, i: (bi, i, qcol + hp)),
            kv_spec("k"), kv_spec("v"),
            pl.BlockSpec((1, 1, tq, 2), lambda bi, hp, i: (bi, hp, i, 0)),
            pl.BlockSpec((1, 1, s // ch, 2, ch), lambda bi, hp, i: (bi, hp, 0, 0, 0)),
        ],
        out_specs=pl.BlockSpec((1, tq, LANES), lambda bi, hp, i: (bi, i, hp)),
        out_shape=jax.ShapeDtypeStruct((b, t, n_pairs * LANES), BF16),
        scratch_shapes=[pltpu.VMEM((2, s // ch, tq, ch), F32)],
        compiler_params=pltpu.CompilerParams(dimension_semantics=("parallel", "parallel", "arbitrary"),
                                             vmem_limit_bytes=VMEM_LIMIT),
        name="fox",
    )(q16, src["k"][0], src["v"][0], cq, ck)


E_QA, E_KA, E_VA, E_IQ, E_QB, E_IK, E_CK, E_CV, E_SK, E_SV, E_WK, E_WV, E_MISC = 0, 4, 5, 6, 8, 12, 13, 14, 15, 16, 17, 18, 19
G_QA, G_KA, G_QB, G_CMP, G_SEL, G_WIN = range(6)
E_KINDS = ((NORM_ROPE, G_QA),) * 4 + ((NORM_ROPE, G_KA), (PLAIN, 0)) + ((ROPE, 0),) * 2 + ((NORM_ROPE, G_QB),) * 4 + (
    (ROPE, 0), (NORM_ROPE, G_CMP), (PLAIN, 0), (NORM_ROPE, G_SEL), (PLAIN, 0), (NORM_ROPE, G_WIN), (PLAIN, 0), (PLAIN, 0))
O_Q, O_K, O_V, O_MISC = 0, 8, 16, 24
O_KINDS = ((NORM, 0),) * 8 + ((NORM, 1),) * 8 + ((PLAIN, 0),) * 8 + ((LOGSIG, 0),)


def _pair_heads(w, n_groups):
    d = w.shape[0]
    nj = w.shape[1] // (n_groups * HEAD_DIM)
    return w.reshape(d, n_groups, nj, HEAD_DIM).transpose(0, 2, 1, 3).reshape(d, -1)


def _even_weight(w_in):
    d = w_in.shape[0]
    sizes = (512, 128, 128, 256, 64, 4, 512, 128, 128, 128, 128, 128, 128, 24)
    offs = np.cumsum((0,) + sizes)
    qa, ka, va, iq, ik, iw, qb, ck, cv, sk, sv, wk, wv, gl = (w_in[:, offs[i]:offs[i + 1]] for i in range(14))
    misc = jnp.concatenate([iw, gl, jnp.zeros((d, LANES - 28), w_in.dtype)], axis=1)
    cols = [_pair_heads(qa, A_KV), ka, va, iq, _pair_heads(qb, B_KV), ik, ik, ck, cv, sk, sv, wk, wv, misc]
    return jnp.concatenate(cols, axis=1).astype(BF16)


def _odd_weight(w_in):
    d = w_in.shape[0]
    n = O_MISC * LANES
    fl = w_in[:, n:]
    return jnp.concatenate([w_in[:, :n], fl, jnp.zeros((d, LANES - fl.shape[1]), w_in.dtype)], axis=1).astype(BF16)


def _gain_table(gains):
    rows = [jnp.tile(g, 2) for g in gains]
    rows += [jnp.zeros((LANES,), F32)] * (8 - len(rows))
    return jnp.stack(rows)


def _rope_tables(pos):
    half = HEAD_DIM // 2
    inv = ROPE_THETA ** (-jnp.arange(half, dtype=F32) / half)
    ang = pos.astype(F32)[:, None] * inv[None, :]
    cos, sin = jnp.cos(ang), jnp.sin(ang)
    return jnp.tile(jnp.concatenate([cos, cos], axis=1), (1, 2)), jnp.tile(jnp.concatenate([-sin, sin], axis=1), (1, 2))


def _pair_rows(w_out_half, n_groups):
    nj = w_out_half.shape[0] // (n_groups * HEAD_DIM)
    return w_out_half.reshape(n_groups, nj, HEAD_DIM, -1).transpose(1, 0, 2, 3).reshape(w_out_half.shape)


def _compress_weight(w):
    w3 = w.reshape(CMP_LEN, HEAD_DIM, HEAD_DIM)
    eye = jnp.eye(B_KV, dtype=w.dtype)
    w5 = w3[:, None, :, None, :] * eye[None, :, None, :, None]
    return w5.reshape(CMP_LEN * B_KV * HEAD_DIM, B_KV * HEAD_DIM).astype(BF16)


def _importance_matrix(nc, ncp, ns, nsp):
    r, m = SEL_LEN // CMP_STRIDE, CMP_LEN // CMP_STRIDE
    mat = np.zeros((ncp, nsp), np.float32)
    for n in range(ns):
        for a in range(r):
            for b_ in range(m):
                c = n * r - m + 1 + a + b_
                if 0 <= c < nc:
                    mat[c, n] += 1.0
    return jnp.asarray(mat)


def _expand_matrix(nsp, s, ch):
    blk = np.arange(s) // SEL_LEN
    mat = (np.arange(nsp)[:, None] == blk[None, :]).astype(np.float32)
    return jnp.asarray(mat.reshape(nsp, s // ch, ch).transpose(1, 0, 2), dtype=BF16)


def _chunk(a, c, n=1):
    return a[..., c * LANES:(c + n) * LANES]


def _pad_rows(a, rows):
    return jnp.pad(a, ((0, 0), (0, rows - a.shape[1]), (0, 0)))


def kernel(x_prompt, x_sample, cache_a_k, cache_a_v, cache_a_idx, cache_b_cmp_k, cache_b_cmp_v, cache_b_sel_k, cache_b_sel_v, state_b_win_k, state_b_win_v, cache_c_k, cache_c_v, cache_c_logf, page_table, norm_mix, norm_ffn, e_w_in, e_qn_a, e_kn_a, e_qn_b, e_kn_cmp, e_kn_sel, e_kn_win, e_cmp_pos, e_w_cmp_k, e_w_cmp_v, e_w_out, o_w_in, o_b_f, o_qn, o_kn, o_w_out, w_up, w_down):
    bp, t, d = x_prompt.shape
    bs, ts, _ = x_sample.shape
    depth = norm_mix.shape[0]
    n_pages = page_table.shape[1]
    past = n_pages * PAGE_SIZE
    n_keys_s = past + ts
    s_s = past + KEY_CHUNK
    assert ts <= TQ_SAMPLE and t % KEY_CHUNK == 0 and past % KEY_CHUNK == 0
    n_buf = state_b_win_k.shape[2]
    c_heads = cache_c_k.shape[3]

    xp = x_prompt.reshape(bp * t, d)
    xs = x_sample.reshape(bs * ts, d)
    cos_p, sin_p = _rope_tables(jnp.arange(t))
    cos_s, sin_s = (jnp.tile(a, (bs, 1)) for a in _rope_tables(past + jnp.arange(ts)))
    zero_bias = jnp.zeros((1, LANES), F32)

    def flat_cache(c):
        return c.reshape(c.shape[:3] + (-1,))

    def nsa_consts(n_keys, s):
        nc = (n_keys - CMP_LEN) // CMP_STRIDE + 1
        ncp = _round_up(nc, LANES)
        ns = -(-n_keys // SEL_LEN)
        nsp = _round_up(ns, LANES)
        return ncp, min(SEL_TOP, ns), _importance_matrix(nc, ncp, ns, nsp), _expand_matrix(nsp, s, KEY_CHUNK)

    def fit_rows(a, rows):
        return a[:, :rows] if a.shape[1] >= rows else _pad_rows(a, rows)

    ev_p, ev_s, od_p, od_s = [], [], [], []
    for li in range(depth):
        j = li // 2
        if li % 2 == 0:
            w = _even_weight(e_w_in[j])
            gains = _gain_table([e_qn_a[j], e_kn_a[j], e_qn_b[j], e_kn_cmp[j], e_kn_sel[j], e_kn_win[j]])
            hp32, hp16 = _inproj(xp, norm_mix[li], w, gains, cos_p, sin_p, zero_bias, E_KINDS, TM_ROWS)
            hs32, hs16 = _inproj(xs, norm_mix[li], w, gains, cos_s, sin_s, zero_bias, E_KINDS, bs * ts)
            hp32, hp16 = hp32.reshape(bp, t, -1), hp16.reshape(bp, t, -1)
            hs32, hs16 = hs32.reshape(bs, ts, -1), hs16.reshape(bs, ts, -1)
            wk2, wv2 = _compress_weight(e_w_cmp_k[j]), _compress_weight(e_w_cmp_v[j])
            pe2 = jnp.pad(jnp.tile(e_cmp_pos[j][:, None, :], (1, B_KV, 1)).reshape(1, -1),
                          ((0, BF16_SUBLANES - 1), (0, 0))).astype(BF16)
            cu = CMP_STRIDE * LANES
            col = dict(qa=E_QA, iq=E_IQ, misc=E_MISC, qb=E_QB)

            st_p = [_chunk(hp32, c) for c in (E_KA, E_VA)]
            st_p.append(_chunk(hp32, E_IK)[..., :IDX_DIM])
            st_p += [_chunk(hp32, c) for c in (E_CK, E_CV, E_SK, E_SV)]
            n_keep = min(WINDOW, t)
            st_p += [_chunk(hp32, c)[:, t - n_keep:] for c in (E_WK, E_WV)]
            ncp, n_sel, imp_mat, exp_mat = nsa_consts(t, t)
            cmpk, cmpv = _compress(_chunk(hp16, E_CK).reshape(bp, t // CMP_STRIDE, cu),
                                   _chunk(hp16, E_CV).reshape(bp, t // CMP_STRIDE, cu), wk2, wv2, pe2)
            src = {name: (hp16, c) for name, c in (("k", E_KA), ("v", E_VA), ("ik", E_IK), ("sk", E_SK),
                                                    ("sv", E_SV), ("wk", E_WK), ("wv", E_WV))}
            oa_p = _dsa(hp16, hp32, src, col, TQ_PROMPT, min(DSA_TOPK, t // 4), 0, t, False)
            ob_p = _nsa(hp16, hp32, fit_rows(cmpk, ncp), fit_rows(cmpv, ncp), src, imp_mat, exp_mat,
                        col, TQ_PROMPT, n_sel, 0, 0, t, False)

            st_s = [_chunk(hs32, c) for c in (E_KA, E_VA)]
            st_s.append(_chunk(hs32, E_IK)[..., :IDX_DIM])
            st_s += [_chunk(hs32, c) for c in (E_CK, E_CV, E_SK, E_SV)]
            wk_all = jnp.concatenate([state_b_win_k[j].reshape(bs, n_buf, -1), _chunk(hs32, E_WK)], axis=1)
            wv_all = jnp.concatenate([state_b_win_v[j].reshape(bs, n_buf, -1), _chunk(hs32, E_WV)], axis=1)
            n_keep = min(WINDOW, n_buf + ts)
            st_s += [wk_all[:, n_buf + ts - n_keep:], wv_all[:, n_buf + ts - n_keep:]]
            caches = [flat_cache(c) for c in (cache_a_k, cache_a_v, cache_a_idx, cache_b_cmp_k, cache_b_cmp_v,
                                              cache_b_sel_k, cache_b_sel_v)]
            news = [_pad_rows(a, TQ_SAMPLE) for a in st_s[:7]]
            gk, gv, gik, gck, gcv, gsk, gsv = _gather(
                page_table, caches, j, news, (False, False, True, False, False, False, False), (BF16,) * 7, GATHER_PAGES)
            qs16, qs32 = _pad_rows(hs16, TQ_SAMPLE), _pad_rows(hs32, TQ_SAMPLE)
            ncp, n_sel, imp_mat, exp_mat = nsa_consts(n_keys_s, s_s)
            cmpk, cmpv = _compress(gck.reshape(bs, s_s // CMP_STRIDE, cu), gcv.reshape(bs, s_s // CMP_STRIDE, cu),
                                   wk2, wv2, pe2)
            sw = _round_up(n_buf + ts, WIN_CHUNK)
            wk_s, wv_s = _pad_rows(wk_all, sw).astype(BF16), _pad_rows(wv_all, sw).astype(BF16)
            src = dict(k=(gk, 0), v=(gv, 0), ik=(gik, 0), sk=(gsk, 0), sv=(gsv, 0), wk=(wk_s, 0), wv=(wv_s, 0))
            oa_s = _dsa(qs16, qs32, src, col, TQ_SAMPLE, min(DSA_TOPK, n_keys_s // 4), past, n_keys_s, True)
            ob_s = _nsa(qs16, qs32, fit_rows(cmpk, ncp), fit_rows(cmpv, ncp), src, imp_mat, exp_mat,
                        col, TQ_SAMPLE, n_sel, past, past - n_buf, n_buf + ts, True)
            ev_p.append(st_p)
            ev_s.append(st_s)
            half = e_w_out.shape[1] // 2
            w_outs = [_pair_rows(e_w_out[j][:half], A_KV).astype(BF16), _pair_rows(e_w_out[j][half:], B_KV).astype(BF16)]
            os_p = [oa_p.reshape(bp * t, -1), ob_p.reshape(bp * t, -1)]
            os_s = [oa_s[:, :ts].reshape(bs * ts, -1), ob_s[:, :ts].reshape(bs * ts, -1)]
        else:
            w = _odd_weight(o_w_in[j])
            gains = _gain_table([o_qn[j], o_kn[j]])
            bias = jnp.pad(o_b_f[j], (0, LANES - c_heads)).reshape(1, LANES)
            hp32, hp16 = _inproj(xp, norm_mix[li], w, gains, cos_p, sin_p, bias, O_KINDS, TM_ROWS)
            hs32, hs16 = _inproj(xs, norm_mix[li], w, gains, cos_s, sin_s, bias, O_KINDS, bs * ts)
            hp32, hp16 = hp32.reshape(bp, t, -1), hp16.reshape(bp, t, -1)
            hs32, hs16 = hs32.reshape(bs, ts, -1), hs16.reshape(bs, ts, -1)
            n_pairs = c_heads // 2

            def fox_bias(cum, qlo, tq_pad):
                b_, _, s_ = cum.shape
                cq = cum[:, :, qlo:qlo + tq_pad].reshape(b_, n_pairs, 2, tq_pad).transpose(0, 1, 3, 2)
                ck = cum.reshape(b_, n_pairs, 2, s_ // KEY_CHUNK, KEY_CHUNK).transpose(0, 1, 3, 2, 4)
                return cq, ck

            logf_p = _chunk(hp32, O_MISC)[..., :c_heads]
            st_p = [_chunk(hp32, O_K, 8), _chunk(hp32, O_V, 8), logf_p]
            cum_p = _cumsum(logf_p.transpose(0, 2, 1))
            cq, ck = fox_bias(cum_p, 0, t)
            o_p = _fox(hp16, dict(k=(hp16, O_K), v=(hp16, O_V)), cq, ck, O_Q, TQ_PROMPT, 0, t, False)

            logf_s = _chunk(hs32, O_MISC)[..., :c_heads]
            st_s = [_chunk(hs32, O_K, 8), _chunk(hs32, O_V, 8), logf_s]
            gk, gv = _gather(page_table, [flat_cache(cache_c_k), flat_cache(cache_c_v)], j,
                             [_pad_rows(st_s[0], TQ_SAMPLE), _pad_rows(st_s[1], TQ_SAMPLE)],
                             (False, False), (BF16, BF16), GATHER_PAGES)
            logf_pages = cache_c_logf.reshape(cache_c_logf.shape[:2] + (1, -1))
            (glogf,) = _gather(page_table, [logf_pages], j, None, (False,), (F32,), LOGF_PAGES)
            logf_all = jnp.concatenate([glogf.reshape(bs, past, c_heads), logf_s], axis=1)
            cum_s = _cumsum(_pad_rows(logf_all, s_s).transpose(0, 2, 1))
            cq, ck = fox_bias(cum_s, past, TQ_SAMPLE)
            o_s = _fox(_pad_rows(hs16, TQ_SAMPLE), dict(k=(gk, 0), v=(gv, 0)), cq, ck, O_Q, TQ_SAMPLE, past, n_keys_s, True)
            od_p.append(st_p)
            od_s.append(st_s)
            w_outs = [o_w_out[j].astype(BF16)]
            os_p = [o_p.reshape(bp * t, -1)]
            os_s = [o_s[:, :ts].reshape(bs * ts, -1)]
        wu, wd = w_up[li].astype(BF16), w_down[li].astype(BF16)
        xp = _postmix(xp, os_p, w_outs, norm_ffn[li], wu, wd, TM_ROWS)
        xs = _postmix(xs, os_s, w_outs, norm_ffn[li], wu, wd, bs * ts)

    outs = [xp.reshape(bp, t, d), xs.reshape(bs, ts, d)]
    n_even = len(ev_p)
    for n in range(9):
        for states, b_ in ((ev_p, bp), (ev_s, bs)):
            rows = states[0][n].shape[1]
            tail = (IDX_DIM,) if n == 2 else (A_KV, HEAD_DIM)
            outs.append(jnp.stack([s[n] for s in states]).reshape((n_even, b_, rows) + tail))
    for n in range(3):
        for states, b_, t_ in ((od_p, bp, t), (od_s, bs, ts)):
            tail = (c_heads,) if n == 2 else (c_heads, HEAD_DIM)
            outs.append(jnp.stack([s[n] for s in states]).reshape((len(states), b_, t_) + tail))
    return tuple(outs)
```

```python
import functools

import numpy as np
import jax
import jax.numpy as jnp
from jax import lax
from jax.experimental import pallas as pl
from jax.experimental.pallas import tpu as pltpu

F32 = jnp.float32
BF16 = jnp.bfloat16

HEAD_DIM = 64
A_KV = 2
IDX_HEADS = 4
IDX_DIM = 64
DSA_TOPK = 256
B_KV = 2
CMP_LEN = 32
CMP_STRIDE = 16
SEL_LEN = 64
SEL_TOP = 16
WINDOW = 512
ROPE_THETA = 10000.0
RMS_EPS = 1e-6
NEG_BIG = -1e30
FORCE_SCORE = 1e9
PAGE_SIZE = 128

LANES = 128
BF16_SUBLANES = 16
VMEM_LIMIT = 56 * 1024 * 1024

GATHER_PAGES = 4
KEY_CHUNK = GATHER_PAGES * PAGE_SIZE
WIN_CHUNK = 256
ATT_UNROLL = 2
TOPK_ROWS = 128
TQ_PROMPT = 256
TQ_SAMPLE = BF16_SUBLANES
TM_ROWS = 256
FF_CHUNK = 1024

PLAIN, ROPE, NORM_ROPE, NORM, LOGSIG = range(5)

INT_MIN = np.int32(-2 ** 31)
KEY_NINF = np.int32(-2139095041)


def _dot_nt(a, b):
    return lax.dot_general(a, b, (((1,), (1,)), ((), ())), preferred_element_type=F32)


def _iota(shape, dim):
    return lax.broadcasted_iota(jnp.int32, shape, dim)


def _round_up(x, m):
    return -(-x // m) * m


def _div_pow2(x, n):
    assert n & (n - 1) == 0
    if isinstance(x, int):
        return x // n
    return lax.shift_right_arithmetic(x, jnp.int32(n.bit_length() - 1))


def _loop(lo, hi, body, init, static, unroll=1):
    if static:
        carry = init
        for c in range(lo, hi):
            carry = body(c, carry)
        return carry
    if unroll == 1:
        return lax.fori_loop(lo, hi, body, init)

    def trip(i, carry):
        for u in range(unroll):
            carry = body(i * unroll + u, carry)
        return carry

    return lax.fori_loop(_div_pow2(lo, unroll), _div_pow2(hi, unroll), trip, init)


def _round_up_pow2(x, n):
    return _div_pow2(x + (n - 1), n) * n


def _min(a, b):
    return min(a, b) if isinstance(a, int) and isinstance(b, int) else jnp.minimum(a, b)


def _max(a, b):
    return max(a, b) if isinstance(a, int) and isinstance(b, int) else jnp.maximum(a, b)


def _chunk_start(c, ch):
    return c * ch if isinstance(c, int) else pl.multiple_of(c * ch, ch)


def _inproj_body(x_ref, g_ref, w_ref, gains_ref, cos_ref, sin_ref, bias_ref, o32_ref, o16_ref, *, kinds):
    x = x_ref[...]
    xn = x * lax.rsqrt(jnp.mean(x * x, axis=-1, keepdims=True) + RMS_EPS) * g_ref[...]
    xb = xn.astype(BF16)
    cos = cos_ref[...]
    sin = sin_ref[...]
    lane = _iota((1, LANES), 1)
    lo = lane < HEAD_DIM
    first_half = (lane & (HEAD_DIM - 1)) < (HEAD_DIM // 2)
    n_chunks = len(kinds)
    group = 4
    for c0 in range(0, n_chunks, group):
        c1 = min(c0 + group, n_chunks)
        hh = jnp.dot(xb, w_ref[:, c0 * LANES:c1 * LANES], preferred_element_type=F32)
        for c in range(c0, c1):
            h = hh[:, (c - c0) * LANES:(c - c0 + 1) * LANES]
            kind, gi = kinds[c]
            if kind in (NORM_ROPE, NORM):
                ss = h * h
                s_lo = jnp.sum(jnp.where(lo, ss, 0.0), axis=-1, keepdims=True)
                s_hi = jnp.sum(jnp.where(lo, 0.0, ss), axis=-1, keepdims=True)
                ms = jnp.where(lo, s_lo, s_hi) * (1.0 / HEAD_DIM)
                h = h * lax.rsqrt(ms + RMS_EPS) * gains_ref[gi:gi + 1, :]
            if kind in (NORM_ROPE, ROPE):
                partner = jnp.where(first_half, pltpu.roll(h, LANES - HEAD_DIM // 2, 1),
                                    pltpu.roll(h, HEAD_DIM // 2, 1))
                h = h * cos + partner * sin
            if kind == LOGSIG:
                z = h + bias_ref[...]
                h = -(jnp.maximum(-z, 0.0) + jnp.log1p(jnp.exp(-jnp.abs(z))))
            o32_ref[:, c * LANES:(c + 1) * LANES] = h
            o16_ref[:, c * LANES:(c + 1) * LANES] = h.astype(BF16)


def _inproj(x, g, w, gains, cos, sin, bias, kinds, tm):
    n, d = x.shape
    nout = w.shape[1]
    n_tab = cos.shape[0] // tm
    const = lambda i: (0, 0)
    return pl.pallas_call(
        functools.partial(_inproj_body, kinds=kinds),
        grid=(n // tm,),
        in_specs=[
            pl.BlockSpec((tm, d), lambda i: (i, 0)),
            pl.BlockSpec((1, d), const),
            pl.BlockSpec((d, nout), const, pipeline_mode=pl.Buffered(1)),
            pl.BlockSpec(gains.shape, const),
            pl.BlockSpec((tm, LANES), lambda i: (i % n_tab, 0)),
            pl.BlockSpec((tm, LANES), lambda i: (i % n_tab, 0)),
            pl.BlockSpec((1, LANES), const),
        ],
        out_specs=[pl.BlockSpec((tm, nout), lambda i: (i, 0)),
                   pl.BlockSpec((tm, nout), lambda i: (i, 0))],
        out_shape=[jax.ShapeDtypeStruct((n, nout), F32), jax.ShapeDtypeStruct((n, nout), BF16)],
        compiler_params=pltpu.CompilerParams(dimension_semantics=("parallel",), vmem_limit_bytes=VMEM_LIMIT),
        name="inproj",
    )(x, g.reshape(1, d), w, gains, cos, sin, bias)


def _postmix_body(*refs, n_o):
    x_ref = refs[0]
    o_refs = refs[1:1 + n_o]
    w_ref, g_ref, wup_ref, wdn_ref, out_ref = refs[1 + n_o:]
    o = o_refs[0][...] if n_o == 1 else jnp.concatenate([r[...] for r in o_refs], axis=-1)
    x1 = x_ref[...] + jnp.dot(o, w_ref[...], preferred_element_type=F32)
    xn = x1 * lax.rsqrt(jnp.mean(x1 * x1, axis=-1, keepdims=True) + RMS_EPS) * g_ref[...]
    xb = xn.astype(BF16)
    acc = x1
    d_ff = wup_ref.shape[1]
    for c in range(d_ff // FF_CHUNK):
        h = jnp.dot(xb, wup_ref[:, c * FF_CHUNK:(c + 1) * FF_CHUNK], preferred_element_type=F32)
        a = jnp.square(jnp.maximum(h, 0.0)).astype(BF16)
        acc = acc + jnp.dot(a, wdn_ref[c * FF_CHUNK:(c + 1) * FF_CHUNK, :], preferred_element_type=F32)
    out_ref[...] = acc


def _postmix(x, os_, ws, g, w_up, w_down, tm):
    n, d = x.shape
    const = lambda i: (0, 0)
    row = lambda i: (i, 0)
    w_out = ws[0] if len(ws) == 1 else jnp.concatenate(ws, axis=0)
    in_specs = [pl.BlockSpec((tm, d), row)]
    in_specs += [pl.BlockSpec((tm, o.shape[1]), row) for o in os_]
    in_specs += [pl.BlockSpec(w_out.shape, const, pipeline_mode=pl.Buffered(1)),
                 pl.BlockSpec((1, d), const),
                 pl.BlockSpec(w_up.shape, const, pipeline_mode=pl.Buffered(1)),
                 pl.BlockSpec(w_down.shape, const, pipeline_mode=pl.Buffered(1))]
    return pl.pallas_call(
        functools.partial(_postmix_body, n_o=len(os_)),
        grid=(n // tm,),
        in_specs=in_specs,
        out_specs=pl.BlockSpec((tm, d), row),
        out_shape=jax.ShapeDtypeStruct((n, d), F32),
        compiler_params=pltpu.CompilerParams(dimension_semantics=("parallel",), vmem_limit_bytes=VMEM_LIMIT),
        name="postmix",
    )(x, *os_, w_out, g.reshape(1, d), w_up, w_down)


def _gather_body(pt_ref, *refs, n, pps, n_steps, dup, has_new):
    del pt_ref
    page_refs = refs[:n * pps]
    new_refs = refs[n * pps:n * pps + (n if has_new else 0)]
    out_refs = refs[n * pps + len(new_refs):]
    rows = page_refs[0].shape[0]

    def widen(val, a):
        return jnp.concatenate([val, val], axis=-1) if dup[a] else val

    def copy_pages():
        for a in range(n):
            for u in range(pps):
                out_refs[a][u * rows:(u + 1) * rows, :] = widen(page_refs[a * pps + u][...], a).astype(out_refs[a].dtype)

    if not has_new:
        copy_pages()
        return
    p = pl.program_id(1)
    pl.when(p < n_steps)(copy_pages)

    @pl.when(p == n_steps)
    def _():
        for a in range(n):
            new = widen(new_refs[a][...], a)
            r = new.shape[0]
            out_refs[a][0:r, :] = new.astype(out_refs[a].dtype)
            out_refs[a][r:, :] = jnp.zeros((pps * rows - r, new.shape[1]), out_refs[a].dtype)


def _gather(page_table, caches, layer, news, dup, out_dtypes, pps):
    bs, n_pages = page_table.shape
    assert n_pages % pps == 0
    n_steps = n_pages // pps
    n = len(caches)
    has_new = news is not None
    rows = caches[0].shape[2]
    in_specs, out_specs, out_shape = [], [], []
    for c in caches:
        for u in range(pps):
            in_specs.append(pl.BlockSpec(
                (None, None, rows, c.shape[-1]),
                lambda b, p, pt, u=u: (layer, pt[b, jnp.minimum(p, n_steps - 1) * pps + u], 0, 0)))
    if has_new:
        for nw in news:
            in_specs.append(pl.BlockSpec((None,) + nw.shape[1:], lambda b, p, pt: (b, 0, 0)))
    for c, d_, dt in zip(caches, dup, out_dtypes):
        width = c.shape[-1] * (2 if d_ else 1)
        out_specs.append(pl.BlockSpec((None, pps * rows, width), lambda b, p, pt: (b, p, 0)))
        out_shape.append(jax.ShapeDtypeStruct((bs, (n_steps + has_new) * pps * rows, width), dt))
    args = [a for c in caches for a in [c] * pps] + (list(news) if has_new else [])
    return pl.pallas_call(
        functools.partial(_gather_body, n=n, pps=pps, n_steps=n_steps, dup=tuple(dup), has_new=has_new),
        grid_spec=pltpu.PrefetchScalarGridSpec(
            num_scalar_prefetch=1, grid=(bs, n_steps + has_new), in_specs=in_specs, out_specs=out_specs),
        out_shape=out_shape,
        compiler_params=pltpu.CompilerParams(dimension_semantics=("parallel", "arbitrary"),
                                             vmem_limit_bytes=VMEM_LIMIT),
        name="paged_gather",
    )(page_table, *args)


def _gather_t_body(pt_ref, *refs, n, pps, n_steps, dup):
    del pt_ref
    page_refs = refs[:n * pps]
    new_refs = refs[n * pps:n * pps + n]
    out_refs = refs[n * pps + n:]

    def widen(val, a):
        return jnp.concatenate([val, val], axis=0) if dup[a] else val

    p = pl.program_id(1)

    @pl.when(p < n_steps)
    def _():
        for a in range(n):
            for u in range(pps):
                out_refs[a][:, u * PAGE_SIZE:(u + 1) * PAGE_SIZE] = widen(page_refs[a * pps + u][...], a).astype(
                    out_refs[a].dtype)

    @pl.when(p == n_steps)
    def _():
        for a in range(n):
            new = widen(new_refs[a][...], a)
            r = new.shape[1]
            out_refs[a][:, 0:r] = new.astype(out_refs[a].dtype)
            out_refs[a][:, r:] = jnp.zeros((new.shape[0], pps * PAGE_SIZE - r), out_refs[a].dtype)


def _gather_t(page_table, caches, layer, news, dup, out_dtypes, pps):
    bs, n_pages = page_table.shape
    assert n_pages % pps == 0
    n_steps = n_pages // pps
    n = len(caches)
    in_specs, out_specs, out_shape = [], [], []
    for c in caches:
        for u in range(pps):
            in_specs.append(pl.BlockSpec(
                (None, None, c.shape[2], PAGE_SIZE),
                lambda b, p, pt, u=u: (layer, pt[b, jnp.minimum(p, n_steps - 1) * pps + u], 0, 0)))
    for nw in news:
        in_specs.append(pl.BlockSpec((None,) + nw.shape[1:], lambda b, p, pt: (b, 0, 0)))
    for c, d_, dt in zip(caches, dup, out_dtypes):
        feat = c.shape[2] * (2 if d_ else 1)
        out_specs.append(pl.BlockSpec((None, feat, pps * PAGE_SIZE), lambda b, p, pt: (b, 0, p)))
        out_shape.append(jax.ShapeDtypeStruct((bs, feat, (n_steps + 1) * pps * PAGE_SIZE), dt))
    args = [a for c in caches for a in [c] * pps] + list(news)
    return pl.pallas_call(
        functools.partial(_gather_t_body, n=n, pps=pps, n_steps=n_steps, dup=tuple(dup)),
        grid_spec=pltpu.PrefetchScalarGridSpec(
            num_scalar_prefetch=1, grid=(bs, n_steps + 1), in_specs=in_specs, out_specs=out_specs),
        out_shape=out_shape,
        compiler_params=pltpu.CompilerParams(dimension_semantics=("parallel", "arbitrary"),
                                             vmem_limit_bytes=VMEM_LIMIT),
        name="paged_gather_t",
    )(page_table, *args)


def _float_key(s):
    b = lax.bitcast_convert_type(s + 0.0, jnp.int32)
    return jnp.where(b >= 0, b, b ^ jnp.int32(0x7FFFFFFF))


def _prefix_tri():
    return jnp.where(_iota((LANES, LANES), 0) <= _iota((LANES, LANES), 1), 1.0, 0.0).astype(BF16)


def _topk_mask(key_ref, sel_ref, k, tri, n_used, static, additive):
    _, all_rows, width = key_ref.shape
    pieces = range(width // LANES)
    on, off = (0.0, NEG_BIG) if additive else (1.0, 0.0)
    kf = float(k)
    rows = min(all_rows, TOPK_ROWS)
    for r0 in range(0, all_rows, rows):
        rs = slice(r0, r0 + rows)

        def count(pred):
            def body(c, acc):
                for u in pieces:
                    acc = acc + jnp.where(pred(key_ref[c, rs, u * LANES:(u + 1) * LANES]), 1.0, 0.0)
                return acc
            return jnp.sum(_loop(0, n_used, body, jnp.zeros((rows, LANES), F32), static), axis=-1, keepdims=True)

        zero = jnp.zeros((rows, 1), jnp.int32)
        t = jnp.where(count(lambda kc: kc >= zero) >= kf, zero, jnp.full((rows, 1), INT_MIN, jnp.int32))

        def bit_step(i, t):
            cand = t | jnp.left_shift(jnp.int32(1), 30 - i)
            return jnp.where(count(lambda kc: kc >= cand) >= kf, cand, t)

        t = _loop(0, 31, bit_step, t, not static)
        need = kf - count(lambda kc: kc > t)

        def mark(c, carry):
            for u in pieces:
                kc = key_ref[c, rs, u * LANES:(u + 1) * LANES]
                eq = kc == t
                pc = jnp.dot(jnp.where(eq, 1.0, 0.0).astype(BF16), tri, preferred_element_type=F32) + carry
                sel = ((kc > t) | (eq & (pc <= need))) & (kc > KEY_NINF)
                sel_ref[c, rs, u * LANES:(u + 1) * LANES] = jnp.where(sel, on, off)
                carry = jnp.broadcast_to(pc[:, LANES - 1:LANES], (rows, LANES))
            return carry

        _loop(0, n_used, mark, jnp.zeros((rows, LANES), F32), static)


def _key_chunk(ref, c, ch, kv_t):
    keys = pl.ds(_chunk_start(c, ch), ch)
    return ref[0, :, keys] if kv_t else ref[0, keys, :]


def _qk(q, kc, kv_t):
    return jnp.dot(q, kc, preferred_element_type=F32) if kv_t else _dot_nt(q, kc)


def _attend_pair(q2, k_ref, v_ref, s_ref, bounds, ch, mask_fn, static, unroll=1, kidx=None, kv_t=False):
    c_lo, c_mid, c_hi = bounds
    tq = q2.shape[0]
    pieces = range(ch // LANES)
    lo = _iota((1, LANES), 1) < HEAD_DIM
    zero = jnp.zeros_like(q2)
    qs = (jnp.where(lo, q2, zero), jnp.where(lo, zero, q2))
    chunk_of = lambda ref, c: _key_chunk(ref, c if kidx is None else kidx(c), ch, kv_t)

    def score(c, mvecs, diag):
        kc = chunk_of(k_ref, c)
        out = []
        for half in range(2):
            s = _qk(qs[half], kc, kv_t)
            extra = mask_fn(c, half, diag)
            if extra is not None:
                s = s + extra
            s_ref[half, c] = s
            mvec = mvecs[half]
            for u in pieces:
                mvec = jnp.maximum(mvec, s[:, u * LANES:(u + 1) * LANES])
            out.append(mvec)
        return tuple(out)

    mvecs = (jnp.full((tq, LANES), NEG_BIG, F32),) * 2
    mvecs = _loop(c_lo, c_mid, functools.partial(score, diag=False), mvecs, static, unroll)
    mvecs = _loop(c_mid, c_hi, functools.partial(score, diag=True), mvecs, static, unroll)
    ms = [jnp.max(mv, axis=-1, keepdims=True) for mv in mvecs]

    def accumulate(c, carry):
        vc = chunk_of(v_ref, c)
        out = []
        for half in range(2):
            lvec, acc = carry[2 * half:2 * half + 2]
            p = jnp.exp(s_ref[half, c] - ms[half])
            for u in pieces:
                lvec = lvec + p[:, u * LANES:(u + 1) * LANES]
            pv = _dot_nt(p.astype(BF16), vc) if kv_t else jnp.dot(p.astype(BF16), vc, preferred_element_type=F32)
            out += [lvec, acc + pv]
        return tuple(out)

    res = _loop(c_lo, c_hi, accumulate, (jnp.zeros((tq, LANES), F32),) * 4, static, unroll)
    outs = []
    for half in range(2):
        l = jnp.sum(res[2 * half], axis=-1, keepdims=True)
        outs.append(jnp.where(ms[half] > 0.5 * NEG_BIG, res[2 * half + 1] / jnp.maximum(l, 1e-30), 0.0))
    return jnp.where(lo, outs[0], outs[1])


def _scaled(q, scale):
    return (q.astype(F32) * scale).astype(BF16)


def _query_block(qpos0, tq, axis, static):
    q0 = qpos0 if static else qpos0 + pl.program_id(axis) * tq
    return q0, q0 + _iota((tq, 1), 0)


def _causal_chunks(q0, tq, n_ch, ch):
    c_hi = _min(n_ch, _div_pow2(q0 + tq - 1, ch) + 1)
    c_mid = _min(c_hi, _div_pow2(q0 + 1, ch))
    if isinstance(c_hi, int):
        return c_mid, c_hi
    assert n_ch % ATT_UNROLL == 0
    return _div_pow2(c_mid, ATT_UNROLL) * ATT_UNROLL, _round_up_pow2(c_hi, ATT_UNROLL)


def _dsa_body(qa_ref, iq_ref, misc_ref, k_ref, v_ref, ik_ref, o_ref, key_ref, sel_ref, s_ref,
              *, tq, ch, topk, qpos0, n_keys, static, kv_t):
    n_ch = key_ref.shape[0]
    q0, qpos = _query_block(qpos0, tq, 1, static)
    _, c_hi = _causal_chunks(q0, tq, n_ch, ch)
    lo = _iota((1, LANES), 1) < HEAD_DIM
    iw = misc_ref[0][:, 0:IDX_HEADS]
    tri = _prefix_tri()

    def score_body(c, _):
        ikc = _key_chunk(ik_ref, c, ch, kv_t)
        acc = jnp.zeros((tq, ch), F32)
        for hp in range(IDX_HEADS // 2):
            iqc = iq_ref[0, :, hp * LANES:(hp + 1) * LANES]
            zero = jnp.zeros_like(iqc)
            for half in range(2):
                qm = jnp.where(lo, iqc, zero) if half == 0 else jnp.where(lo, zero, iqc)
                s = _qk(qm, ikc, kv_t) * (IDX_DIM ** -0.5)
                h = 2 * hp + half
                acc = acc + jnp.maximum(s, 0.0) * iw[:, h:h + 1]
        sc = acc * (IDX_HEADS ** -0.5)
        kpos = c * ch + _iota((1, ch), 1)
        key_ref[c] = jnp.where((kpos <= qpos) & (kpos < n_keys), _float_key(sc), KEY_NINF)
        return 0

    _loop(0, c_hi, score_body, 0, static, ATT_UNROLL)
    _topk_mask(key_ref, sel_ref, topk, tri, c_hi, static, additive=True)

    for j in range(qa_ref.shape[2] // LANES):
        q2 = _scaled(qa_ref[0, :, j * LANES:(j + 1) * LANES], HEAD_DIM ** -0.5)
        o = _attend_pair(q2, k_ref, v_ref, s_ref, (0, 0, c_hi), ch, lambda c, half, diag: sel_ref[c],
                         static, ATT_UNROLL, kv_t=kv_t)
        o_ref[0, :, j * LANES:(j + 1) * LANES] = o.astype(o_ref.dtype)


def _kv_len(arr, kv_t):
    return arr.shape[2] if kv_t else arr.shape[1]


def _kv_spec(operand, kv_t, chunk_of_step=lambda *g: 0):
    arr, c0 = operand
    s = _kv_len(arr, kv_t)
    if kv_t:
        return pl.BlockSpec((1, LANES, s), lambda bi, *g: (bi, c0 + chunk_of_step(*g), 0))
    return pl.BlockSpec((1, s, LANES), lambda bi, *g: (bi, 0, c0 + chunk_of_step(*g)))


def _dsa(q16, m32, src, col, tq, topk, qpos0, n_keys, static, kv_t):
    b, t, _ = q16.shape
    s = _kv_len(src["k"][0], kv_t)
    ch = KEY_CHUNK
    n_qa = 4 * LANES
    kv_spec = lambda name: _kv_spec(src[name], kv_t)
    return pl.pallas_call(
        functools.partial(_dsa_body, tq=tq, ch=ch, topk=topk, qpos0=qpos0, n_keys=n_keys, static=static,
                          kv_t=kv_t),
        grid=(b, t // tq),
        in_specs=[
            pl.BlockSpec((1, tq, n_qa), lambda bi, i: (bi, i, col["qa"] * LANES // n_qa)),
            pl.BlockSpec((1, tq, 2 * LANES), lambda bi, i: (bi, i, col["iq"] // 2)),
            pl.BlockSpec((1, tq, LANES), lambda bi, i: (bi, i, col["misc"])),
            kv_spec("k"), kv_spec("v"), kv_spec("ik"),
        ],
        out_specs=pl.BlockSpec((1, tq, n_qa), lambda bi, i: (bi, i, 0)),
        out_shape=jax.ShapeDtypeStruct((b, t, n_qa), BF16),
        scratch_shapes=[pltpu.VMEM((s // ch, tq, ch), jnp.int32), pltpu.VMEM((s // ch, tq, ch), F32),
                        pltpu.VMEM((2, s // ch, tq, ch), F32)],
        compiler_params=pltpu.CompilerParams(dimension_semantics=("parallel", "arbitrary"),
                                             vmem_limit_bytes=VMEM_LIMIT),
        name="dsa",
    )(q16, q16, m32, src["k"][0], src["v"][0], src["ik"][0])


def _nsa_body(qb_ref, misc_ref, cmpk_ref, cmpv_ref, sk_ref, sv_ref, wk_ref, wv_ref, imp_ref, exp_ref,
              o_ref, key_ref, selb_ref, msk_ref, s_ref, sw_ref,
              *, tq, ch, n_sel, qpos0, wpos0, n_wvalid, static, kv_t):
    n_ch = exp_ref.shape[0]
    ncp = cmpk_ref.shape[1]
    nsp = imp_ref.shape[1]
    q0, qpos = _query_block(qpos0, tq, 1, static)
    _, c_hi = _causal_chunks(q0, tq, n_ch, ch)
    lo = _iota((1, LANES), 1) < HEAD_DIM
    scale = HEAD_DIM ** -0.5
    tri = _prefix_tri()
    n_chunks_q = qb_ref.shape[2] // LANES
    gates = jax.nn.sigmoid(misc_ref[0])

    cvalid = (_iota((1, ncp), 1) * CMP_STRIDE + (CMP_LEN - 1)) <= qpos
    ck = cmpk_ref[0]
    cv = cmpv_ref[0]
    imp_c = [jnp.zeros((tq, ncp), F32), jnp.zeros((tq, ncp), F32)]
    o_cmp = []
    for j in range(n_chunks_q):
        q2 = qb_ref[0, :, j * LANES:(j + 1) * LANES]
        zero = jnp.zeros_like(q2)
        outs = []
        for half in range(2):
            qm = jnp.where(lo, q2, zero) if half == 0 else jnp.where(lo, zero, q2)
            s = jnp.where(cvalid, _dot_nt(qm, ck) * scale, NEG_BIG)
            p = jnp.where(cvalid, jnp.exp(s - jnp.max(s, axis=-1, keepdims=True)), 0.0)
            pn = p / jnp.maximum(jnp.sum(p, axis=-1, keepdims=True), 1e-30)
            imp_c[half] = imp_c[half] + pn
            outs.append(jnp.dot(pn.astype(BF16), cv, preferred_element_type=F32))
        o_cmp.append(jnp.where(lo, outs[0], outs[1]))

    blk = _iota((1, nsp), 1)
    adm = blk * SEL_LEN <= qpos
    cur = _div_pow2(qpos, SEL_LEN)
    forced = adm & ((blk == 0) | (blk == cur) | (blk == cur - 1))
    for g in range(2):
        imp = jnp.dot(imp_c[g], imp_ref[...], preferred_element_type=F32, precision=lax.Precision.HIGHEST)
        score = jnp.where(forced, FORCE_SCORE, jnp.where(adm, imp, -jnp.inf))
        key_ref[0] = _float_key(score)
        _topk_mask(key_ref, selb_ref, n_sel, tri, 1, True, additive=False)
        bm = selb_ref[0].astype(BF16)

        def expand(c, _):
            kpos = c * ch + _iota((1, ch), 1)
            hit = (jnp.dot(bm, exp_ref[c], preferred_element_type=F32) > 0.5) & (kpos <= qpos)
            msk_ref[g, c] = jnp.where(hit, 0.0, NEG_BIG)
            return 0

        _loop(0, c_hi, expand, 0, static, ATT_UNROLL)

    wch = WIN_CHUNK
    n_wch = wk_ref.shape[2 if kv_t else 1] // wch
    n_win = sw_ref.shape[1]
    wc0 = _div_pow2(q0 - WINDOW - wpos0, wch)

    def win_chunk(u):
        c = wc0 + u
        return _min(_max(c, 0), n_wch - 1)

    def win_mask(u, half, diag):
        row = (wc0 + u) * wch + _iota((1, wch), 1)
        kpos = wpos0 + row
        dlt = qpos - kpos
        ok = (dlt >= 0) & (dlt <= WINDOW) & (kpos >= 0) & (row >= 0) & (row < n_wvalid)
        return jnp.where(ok, 0.0, NEG_BIG)

    for j in range(n_chunks_q):
        q2 = _scaled(qb_ref[0, :, j * LANES:(j + 1) * LANES], scale)
        o_sel = _attend_pair(q2, sk_ref, sv_ref, s_ref, (0, 0, c_hi), ch, lambda c, half, diag: msk_ref[half, c],
                             static, ATT_UNROLL, kv_t=kv_t)
        o_win = _attend_pair(q2, wk_ref, wv_ref, sw_ref, (0, 0, n_win), wch, win_mask, True, kidx=win_chunk,
                             kv_t=kv_t)
        o = jnp.zeros((tq, LANES), F32)
        for bi, ob in enumerate((o_cmp[j], o_sel, o_win)):
            c_a = IDX_HEADS + 3 * j + bi
            c_b = IDX_HEADS + 3 * (n_chunks_q + j) + bi
            o = o + jnp.where(lo, gates[:, c_a:c_a + 1], gates[:, c_b:c_b + 1]) * ob
        o_ref[0, :, j * LANES:(j + 1) * LANES] = o.astype(o_ref.dtype)


def _nsa(q16, m32, cmpk, cmpv, src, imp_mat, exp_mat, col, tq, n_sel, qpos0, wpos0, n_wvalid, static, kv_t):
    b, t, _ = q16.shape
    s = _kv_len(src["sk"][0], kv_t)
    ch = exp_mat.shape[2]
    ncp, nsp = imp_mat.shape
    n_qb = 4 * LANES
    assert t == tq or tq % WIN_CHUNK == 0
    mis = (qpos0 - WINDOW - wpos0) % WIN_CHUNK
    n_win = (mis + WINDOW + tq - 1) // WIN_CHUNK + 1
    spec = lambda name: _kv_spec(src[name], kv_t)
    return pl.pallas_call(
        functools.partial(_nsa_body, tq=tq, ch=ch, n_sel=n_sel, qpos0=qpos0, wpos0=wpos0, n_wvalid=n_wvalid,
                          static=static, kv_t=kv_t),
        grid=(b, t // tq),
        in_specs=[
            pl.BlockSpec((1, tq, n_qb), lambda bi, i: (bi, i, col["qb"] * LANES // n_qb)),
            pl.BlockSpec((1, tq, LANES), lambda bi, i: (bi, i, col["misc"])),
            pl.BlockSpec((1, ncp, LANES), lambda bi, i: (bi, 0, 0)),
            pl.BlockSpec((1, ncp, LANES), lambda bi, i: (bi, 0, 0)),
            spec("sk"), spec("sv"), spec("wk"), spec("wv"),
            pl.BlockSpec((ncp, nsp), lambda bi, i: (0, 0)),
            pl.BlockSpec(exp_mat.shape, lambda bi, i: (0, 0, 0)),
        ],
        out_specs=pl.BlockSpec((1, tq, n_qb), lambda bi, i: (bi, i, 0)),
        out_shape=jax.ShapeDtypeStruct((b, t, n_qb), BF16),
        scratch_shapes=[pltpu.VMEM((1, tq, nsp), jnp.int32), pltpu.VMEM((1, tq, nsp), F32),
                        pltpu.VMEM((2, s // ch, tq, ch), F32), pltpu.VMEM((2, s // ch, tq, ch), F32),
                        pltpu.VMEM((2, n_win, tq, WIN_CHUNK), F32)],
        compiler_params=pltpu.CompilerParams(dimension_semantics=("parallel", "arbitrary"),
                                             vmem_limit_bytes=VMEM_LIMIT),
        name="nsa",
    )(q16, m32, cmpk, cmpv, src["sk"][0], src["sv"][0], src["wk"][0], src["wv"][0], imp_mat, exp_mat)


def _compress_body(uk_ref, uv_ref, wk_ref, wv_ref, pe_ref, ok_ref, ov_ref, shift_ref):
    nu = uk_ref.shape[1]
    half = wk_ref.shape[0] // 2
    shift_ref[nu:nu + 8, :] = jnp.zeros((8, LANES), F32)
    for u_ref, w_ref, o_ref in ((uk_ref, wk_ref, ok_ref), (uv_ref, wv_ref, ov_ref)):
        u = u_ref[0]
        top = jnp.dot(u, w_ref[0:half, :], preferred_element_type=F32)
        shift_ref[0:nu, :] = jnp.dot(u, w_ref[half:2 * half, :], preferred_element_type=F32)
        pec = jnp.dot(pe_ref[...], w_ref[...], preferred_element_type=F32)[0:1, :]
        o_ref[0] = (top + shift_ref[1:nu + 1, :] + pec).astype(o_ref.dtype)


def _compress(uk, uv, wk2, wv2, pe2):
    b, nu, width = uk.shape
    const = lambda bi: (0, 0)
    row = lambda bi: (bi, 0, 0)
    return pl.pallas_call(
        _compress_body,
        grid=(b,),
        in_specs=[pl.BlockSpec((1, nu, width), row), pl.BlockSpec((1, nu, width), row),
                  pl.BlockSpec(wk2.shape, const), pl.BlockSpec(wv2.shape, const), pl.BlockSpec(pe2.shape, const)],
        out_specs=[pl.BlockSpec((1, nu, LANES), row), pl.BlockSpec((1, nu, LANES), row)],
        out_shape=[jax.ShapeDtypeStruct((b, nu, LANES), BF16)] * 2,
        scratch_shapes=[pltpu.VMEM((nu + 8, LANES), F32)],
        compiler_params=pltpu.CompilerParams(dimension_semantics=("parallel",), vmem_limit_bytes=VMEM_LIMIT),
        name="compress",
    )(uk, uv, wk2, wv2, pe2)


def _cumsum_body(x_ref, o_ref):
    s = x_ref.shape[2]
    tri = jnp.where(_iota((LANES, LANES), 0) <= _iota((LANES, LANES), 1), 1.0, 0.0)
    carry = jnp.zeros((x_ref.shape[1], 1), F32)
    for c in range(s // LANES):
        blk = x_ref[0, :, c * LANES:(c + 1) * LANES]
        pc = jnp.dot(blk, tri, preferred_element_type=F32, precision=lax.Precision.HIGHEST) + carry
        o_ref[0, :, c * LANES:(c + 1) * LANES] = pc
        carry = pc[:, LANES - 1:LANES]


def _cumsum(x):
    b, h, s = x.shape
    return pl.pallas_call(
        _cumsum_body,
        grid=(b,),
        in_specs=[pl.BlockSpec((1, h, s), lambda bi: (bi, 0, 0))],
        out_specs=pl.BlockSpec((1, h, s), lambda bi: (bi, 0, 0)),
        out_shape=jax.ShapeDtypeStruct((b, h, s), F32),
        compiler_params=pltpu.CompilerParams(dimension_semantics=("parallel",)),
        name="cumsum",
    )(x)


def _fox_body(q_ref, k_ref, v_ref, cq_ref, ck_ref, o_ref, s_ref, *, tq, ch, qpos0, n_keys, static, kv_t):
    n_ch = ck_ref.shape[2]
    q0, qpos = _query_block(qpos0, tq, 2, static)
    c_mid, c_hi = _causal_chunks(q0, tq, n_ch, ch)
    cq = cq_ref[0, 0]

    def bias_fn(c, half, diag):
        bias = cq[:, half:half + 1] - ck_ref[0, 0, c, half:half + 1, :]
        if diag:
            kpos = c * ch + _iota((1, ch), 1)
            bias = jnp.where((kpos <= qpos) & (kpos < n_keys), bias, NEG_BIG)
        return bias

    q2 = _scaled(q_ref[0], HEAD_DIM ** -0.5)
    o = _attend_pair(q2, k_ref, v_ref, s_ref, (0, c_mid, c_hi), ch, bias_fn, static, ATT_UNROLL, kv_t=kv_t)
    o_ref[0] = o.astype(o_ref.dtype)


def _fox(q16, src, cq, ck, qcol, tq, qpos0, n_keys, static, kv_t):
    b, t, _ = q16.shape
    s = _kv_len(src["k"][0], kv_t)
    n_pairs = cq.shape[1]
    ch = ck.shape[4]
    kv_spec = lambda name: _kv_spec(src[name], kv_t, lambda hp, i: hp)
    return pl.pallas_call(
        functools.partial(_fox_body, tq=tq, ch=ch, qpos0=qpos0, n_keys=n_keys, static=static, kv_t=kv_t),
        grid=(b, n_pairs, t // tq),
        in_specs=[
            pl.BlockSpec((1, tq, LANES), lambda bi, hp, i: (bi, i, qcol + hp)),
            kv_spec("k"), kv_spec("v"),
            pl.BlockSpec((1, 1, tq, 2), lambda bi, hp, i: (bi, hp, i, 0)),
            pl.BlockSpec((1, 1, s // ch, 2, ch), lambda bi, hp, i: (bi, hp, 0, 0, 0)),
        ],
        out_specs=pl.BlockSpec((1, tq, LANES), lambda bi, hp, i: (bi, i, hp)),
        out_shape=jax.ShapeDtypeStruct((b, t, n_pairs * LANES), BF16),
        scratch_shapes=[pltpu.VMEM((2, s // ch, tq, ch), F32)],
        compiler_params=pltpu.CompilerParams(dimension_semantics=("parallel", "parallel", "arbitrary"),
                                             vmem_limit_bytes=VMEM_LIMIT),
        name="fox",
    )(q16, src["k"][0], src["v"][0], cq, ck)


E_QA, E_KA, E_VA, E_IQ, E_QB, E_IK, E_CK, E_CV, E_SK, E_SV, E_WK, E_WV, E_MISC = 0, 4, 5, 6, 8, 12, 13, 14, 15, 16, 17, 18, 19
G_QA, G_KA, G_QB, G_CMP, G_SEL, G_WIN = range(6)
E_KINDS = ((NORM_ROPE, G_QA),) * 4 + ((NORM_ROPE, G_KA), (PLAIN, 0)) + ((ROPE, 0),) * 2 + ((NORM_ROPE, G_QB),) * 4 + (
    (ROPE, 0), (NORM_ROPE, G_CMP), (PLAIN, 0), (NORM_ROPE, G_SEL), (PLAIN, 0), (NORM_ROPE, G_WIN), (PLAIN, 0), (PLAIN, 0))
O_Q, O_K, O_V, O_MISC = 0, 8, 16, 24
O_KINDS = ((NORM, 0),) * 8 + ((NORM, 1),) * 8 + ((PLAIN, 0),) * 8 + ((LOGSIG, 0),)


def _pair_heads(w, n_groups):
    d = w.shape[0]
    nj = w.shape[1] // (n_groups * HEAD_DIM)
    return w.reshape(d, n_groups, nj, HEAD_DIM).transpose(0, 2, 1, 3).reshape(d, -1)


def _even_weight(w_in):
    d = w_in.shape[0]
    sizes = (512, 128, 128, 256, 64, 4, 512, 128, 128, 128, 128, 128, 128, 24)
    offs = np.cumsum((0,) + sizes)
    qa, ka, va, iq, ik, iw, qb, ck, cv, sk, sv, wk, wv, gl = (w_in[:, offs[i]:offs[i + 1]] for i in range(14))
    misc = jnp.concatenate([iw, gl, jnp.zeros((d, LANES - 28), w_in.dtype)], axis=1)
    cols = [_pair_heads(qa, A_KV), ka, va, iq, _pair_heads(qb, B_KV), ik, ik, ck, cv, sk, sv, wk, wv, misc]
    return jnp.concatenate(cols, axis=1).astype(BF16)


def _odd_weight(w_in):
    d = w_in.shape[0]
    n = O_MISC * LANES
    fl = w_in[:, n:]
    return jnp.concatenate([w_in[:, :n], fl, jnp.zeros((d, LANES - fl.shape[1]), w_in.dtype)], axis=1).astype(BF16)


def _gain_table(gains):
    rows = [jnp.tile(g, 2) for g in gains]
    rows += [jnp.zeros((LANES,), F32)] * (8 - len(rows))
    return jnp.stack(rows)


def _rope_tables(pos):
    half = HEAD_DIM // 2
    inv = ROPE_THETA ** (-jnp.arange(half, dtype=F32) / half)
    ang = pos.astype(F32)[:, None] * inv[None, :]
    cos, sin = jnp.cos(ang), jnp.sin(ang)
    return jnp.tile(jnp.concatenate([cos, cos], axis=1), (1, 2)), jnp.tile(jnp.concatenate([-sin, sin], axis=1), (1, 2))


def _pair_rows(w_out_half, n_groups):
    nj = w_out_half.shape[0] // (n_groups * HEAD_DIM)
    return w_out_half.reshape(n_groups, nj, HEAD_DIM, -1).transpose(1, 0, 2, 3).reshape(w_out_half.shape)


def _compress_weight(w):
    w3 = w.reshape(CMP_LEN, HEAD_DIM, HEAD_DIM)
    eye = jnp.eye(B_KV, dtype=w.dtype)
    w5 = w3[:, None, :, None, :] * eye[None, :, None, :, None]
    return w5.reshape(CMP_LEN * B_KV * HEAD_DIM, B_KV * HEAD_DIM).astype(BF16)


def _importance_matrix(nc, ncp, ns, nsp):
    r, m = SEL_LEN // CMP_STRIDE, CMP_LEN // CMP_STRIDE
    mat = np.zeros((ncp, nsp), np.float32)
    for n in range(ns):
        for a in range(r):
            for b_ in range(m):
                c = n * r - m + 1 + a + b_
                if 0 <= c < nc:
                    mat[c, n] += 1.0
    return jnp.asarray(mat)


def _expand_matrix(nsp, s, ch):
    blk = np.arange(s) // SEL_LEN
    mat = (np.arange(nsp)[:, None] == blk[None, :]).astype(np.float32)
    return jnp.asarray(mat.reshape(nsp, s // ch, ch).transpose(1, 0, 2), dtype=BF16)


def _chunk(a, c, n=1):
    return a[..., c * LANES:(c + n) * LANES]


def _pad_rows(a, rows):
    return jnp.pad(a, ((0, 0), (0, rows - a.shape[1]), (0, 0)))


def kernel(x_prompt, x_sample, cache_a_k, cache_a_v, cache_a_idx, cache_b_cmp_k, cache_b_cmp_v, cache_b_sel_k, cache_b_sel_v, state_b_win_k, state_b_win_v, cache_c_k, cache_c_v, cache_c_logf, page_table, norm_mix, norm_ffn, e_w_in, e_qn_a, e_kn_a, e_qn_b, e_kn_cmp, e_kn_sel, e_kn_win, e_cmp_pos, e_w_cmp_k, e_w_cmp_v, e_w_out, o_w_in, o_b_f, o_qn, o_kn, o_w_out, w_up, w_down):
    bp, t, d = x_prompt.shape
    bs, ts, _ = x_sample.shape
    depth = norm_mix.shape[0]
    n_pages = page_table.shape[1]
    past = n_pages * PAGE_SIZE
    n_keys_s = past + ts
    s_s = past + KEY_CHUNK
    assert ts <= TQ_SAMPLE and t % KEY_CHUNK == 0 and past % KEY_CHUNK == 0
    n_buf = state_b_win_k.shape[2]
    c_heads = cache_c_k.shape[3]

    xp = x_prompt.reshape(bp * t, d)
    xs = x_sample.reshape(bs * ts, d)
    cos_p, sin_p = _rope_tables(jnp.arange(t))
    cos_s, sin_s = (jnp.tile(a, (bs, 1)) for a in _rope_tables(past + jnp.arange(ts)))
    zero_bias = jnp.zeros((1, LANES), F32)

    def flat_cache(c):
        return c.reshape(c.shape[:3] + (-1,))

    def feat_major(c):
        if c.ndim == 5:
            return c.transpose(0, 1, 3, 4, 2).reshape(c.shape[:2] + (c.shape[3] * c.shape[4], c.shape[2]))
        return c.transpose(0, 1, 3, 2)

    def key_major(a, heads):
        return a.reshape(a.shape[0], heads, a.shape[1] // heads, a.shape[2]).transpose(0, 3, 1, 2)

    def nsa_consts(n_keys, s):
        nc = (n_keys - CMP_LEN) // CMP_STRIDE + 1
        ncp = _round_up(nc, LANES)
        ns = -(-n_keys // SEL_LEN)
        nsp = _round_up(ns, LANES)
        return ncp, min(SEL_TOP, ns), _importance_matrix(nc, ncp, ns, nsp), _expand_matrix(nsp, s, KEY_CHUNK)

    def fit_rows(a, rows):
        return a[:, :rows] if a.shape[1] >= rows else _pad_rows(a, rows)

    ev_p, ev_s, od_p, od_s = [], [], [], []
    for li in range(depth):
        j = li // 2
        if li % 2 == 0:
            w = _even_weight(e_w_in[j])
            gains = _gain_table([e_qn_a[j], e_kn_a[j], e_qn_b[j], e_kn_cmp[j], e_kn_sel[j], e_kn_win[j]])
            hp32, hp16 = _inproj(xp, norm_mix[li], w, gains, cos_p, sin_p, zero_bias, E_KINDS, TM_ROWS)
            hs32, hs16 = _inproj(xs, norm_mix[li], w, gains, cos_s, sin_s, zero_bias, E_KINDS, bs * ts)
            hp32, hp16 = hp32.reshape(bp, t, -1), hp16.reshape(bp, t, -1)
            hs32, hs16 = hs32.reshape(bs, ts, -1), hs16.reshape(bs, ts, -1)
            wk2, wv2 = _compress_weight(e_w_cmp_k[j]), _compress_weight(e_w_cmp_v[j])
            pe2 = jnp.pad(jnp.tile(e_cmp_pos[j][:, None, :], (1, B_KV, 1)).reshape(1, -1),
                          ((0, BF16_SUBLANES - 1), (0, 0))).astype(BF16)
            cu = CMP_STRIDE * LANES
            col = dict(qa=E_QA, iq=E_IQ, misc=E_MISC, qb=E_QB)

            st_p = [_chunk(hp32, c) for c in (E_KA, E_VA)]
            st_p.append(_chunk(hp32, E_IK)[..., :IDX_DIM])
            st_p += [_chunk(hp32, c) for c in (E_CK, E_CV, E_SK, E_SV)]
            n_keep = min(WINDOW, t)
            st_p += [_chunk(hp32, c)[:, t - n_keep:] for c in (E_WK, E_WV)]
            ncp, n_sel, imp_mat, exp_mat = nsa_consts(t, t)
            cmpk, cmpv = _compress(_chunk(hp16, E_CK).reshape(bp, t // CMP_STRIDE, cu),
                                   _chunk(hp16, E_CV).reshape(bp, t // CMP_STRIDE, cu), wk2, wv2, pe2)
            src = {name: (hp16, c) for name, c in (("k", E_KA), ("v", E_VA), ("ik", E_IK), ("sk", E_SK),
                                                    ("sv", E_SV), ("wk", E_WK), ("wv", E_WV))}
            oa_p = _dsa(hp16, hp32, src, col, TQ_PROMPT, min(DSA_TOPK, t // 4), 0, t, False, False)
            ob_p = _nsa(hp16, hp32, fit_rows(cmpk, ncp), fit_rows(cmpv, ncp), src, imp_mat, exp_mat,
                        col, TQ_PROMPT, n_sel, 0, 0, t, False, False)

            st_s = [_chunk(hs32, c) for c in (E_KA, E_VA)]
            st_s.append(_chunk(hs32, E_IK)[..., :IDX_DIM])
            st_s += [_chunk(hs32, c) for c in (E_CK, E_CV, E_SK, E_SV)]
            wk_all = jnp.concatenate([feat_major(state_b_win_k)[j], jnp.swapaxes(_chunk(hs32, E_WK), 1, 2)], axis=2)
            wv_all = jnp.concatenate([feat_major(state_b_win_v)[j], jnp.swapaxes(_chunk(hs32, E_WV), 1, 2)], axis=2)
            n_keep = min(WINDOW, n_buf + ts)
            st_s += [key_major(a[:, :, n_buf + ts - n_keep:], B_KV) for a in (wk_all, wv_all)]
            caches = [feat_major(c) for c in (cache_a_k, cache_a_v, cache_a_idx, cache_b_sel_k, cache_b_sel_v)]
            news = [jnp.swapaxes(_pad_rows(st_s[n], TQ_SAMPLE), 1, 2) for n in (0, 1, 2, 5, 6)]
            gk, gv, gik, gsk, gsv = _gather_t(
                page_table, caches, j, news, (False, False, True, False, False), (BF16,) * 5, GATHER_PAGES)
            gck, gcv = _gather(page_table, [flat_cache(cache_b_cmp_k), flat_cache(cache_b_cmp_v)], j,
                               [_pad_rows(st_s[3], TQ_SAMPLE), _pad_rows(st_s[4], TQ_SAMPLE)],
                               (False, False), (BF16, BF16), GATHER_PAGES)
            qs16, qs32 = _pad_rows(hs16, TQ_SAMPLE), _pad_rows(hs32, TQ_SAMPLE)
            ncp, n_sel, imp_mat, exp_mat = nsa_consts(n_keys_s, s_s)
            cmpk, cmpv = _compress(gck.reshape(bs, s_s // CMP_STRIDE, cu), gcv.reshape(bs, s_s // CMP_STRIDE, cu),
                                   wk2, wv2, pe2)
            sw = _round_up(n_buf + ts, WIN_CHUNK)
            pad_keys = lambda a: jnp.pad(a, ((0, 0), (0, 0), (0, sw - a.shape[2]))).astype(BF16)
            src = dict(k=(gk, 0), v=(gv, 0), ik=(gik, 0), sk=(gsk, 0), sv=(gsv, 0),
                       wk=(pad_keys(wk_all), 0), wv=(pad_keys(wv_all), 0))
            oa_s = _dsa(qs16, qs32, src, col, TQ_SAMPLE, min(DSA_TOPK, n_keys_s // 4), past, n_keys_s, True, True)
            ob_s = _nsa(qs16, qs32, fit_rows(cmpk, ncp), fit_rows(cmpv, ncp), src, imp_mat, exp_mat,
                        col, TQ_SAMPLE, n_sel, past, past - n_buf, n_buf + ts, True, True)
            ev_p.append(st_p)
            ev_s.append(st_s)
            half = e_w_out.shape[1] // 2
            w_outs = [_pair_rows(e_w_out[j][:half], A_KV).astype(BF16), _pair_rows(e_w_out[j][half:], B_KV).astype(BF16)]
            os_p = [oa_p.reshape(bp * t, -1), ob_p.reshape(bp * t, -1)]
            os_s = [oa_s[:, :ts].reshape(bs * ts, -1), ob_s[:, :ts].reshape(bs * ts, -1)]
        else:
            w = _odd_weight(o_w_in[j])
            gains = _gain_table([o_qn[j], o_kn[j]])
            bias = jnp.pad(o_b_f[j], (0, LANES - c_heads)).reshape(1, LANES)
            hp32, hp16 = _inproj(xp, norm_mix[li], w, gains, cos_p, sin_p, bias, O_KINDS, TM_ROWS)
            hs32, hs16 = _inproj(xs, norm_mix[li], w, gains, cos_s, sin_s, bias, O_KINDS, bs * ts)
            hp32, hp16 = hp32.reshape(bp, t, -1), hp16.reshape(bp, t, -1)
            hs32, hs16 = hs32.reshape(bs, ts, -1), hs16.reshape(bs, ts, -1)
            n_pairs = c_heads // 2

            def fox_bias(cum, qlo, tq_pad):
                b_, _, s_ = cum.shape
                cq = cum[:, :, qlo:qlo + tq_pad].reshape(b_, n_pairs, 2, tq_pad).transpose(0, 1, 3, 2)
                ck = cum.reshape(b_, n_pairs, 2, s_ // KEY_CHUNK, KEY_CHUNK).transpose(0, 1, 3, 2, 4)
                return cq, ck

            logf_p = _chunk(hp32, O_MISC)[..., :c_heads]
            st_p = [_chunk(hp32, O_K, 8), _chunk(hp32, O_V, 8), logf_p]
            cum_p = _cumsum(logf_p.transpose(0, 2, 1))
            cq, ck = fox_bias(cum_p, 0, t)
            o_p = _fox(hp16, dict(k=(hp16, O_K), v=(hp16, O_V)), cq, ck, O_Q, TQ_PROMPT, 0, t, False, False)

            logf_s = _chunk(hs32, O_MISC)[..., :c_heads]
            st_s = [_chunk(hs32, O_K, 8), _chunk(hs32, O_V, 8), logf_s]
            news = [jnp.swapaxes(_pad_rows(a, TQ_SAMPLE), 1, 2) for a in st_s]
            gk, gv = _gather_t(page_table, [feat_major(cache_c_k), feat_major(cache_c_v)], j, news[:2],
                               (False, False), (BF16, BF16), GATHER_PAGES)
            (glogf,) = _gather_t(page_table, [feat_major(cache_c_logf)], j, news[2:], (False,), (F32,), GATHER_PAGES)
            cum_s = _cumsum(glogf)
            cq, ck = fox_bias(cum_s, past, TQ_SAMPLE)
            o_s = _fox(_pad_rows(hs16, TQ_SAMPLE), dict(k=(gk, 0), v=(gv, 0)), cq, ck, O_Q, TQ_SAMPLE, past, n_keys_s,
                       True, True)
            od_p.append(st_p)
            od_s.append(st_s)
            w_outs = [o_w_out[j].astype(BF16)]
            os_p = [o_p.reshape(bp * t, -1)]
            os_s = [o_s[:, :ts].reshape(bs * ts, -1)]
        wu, wd = w_up[li].astype(BF16), w_down[li].astype(BF16)
        xp = _postmix(xp, os_p, w_outs, norm_ffn[li], wu, wd, TM_ROWS)
        xs = _postmix(xs, os_s, w_outs, norm_ffn[li], wu, wd, bs * ts)

    outs = [xp.reshape(bp, t, d), xs.reshape(bs, ts, d)]
    n_even = len(ev_p)
    for n in range(9):
        for states, b_ in ((ev_p, bp), (ev_s, bs)):
            rows = states[0][n].shape[1]
            tail = (IDX_DIM,) if n == 2 else (A_KV, HEAD_DIM)
            outs.append(jnp.stack([s[n] for s in states]).reshape((n_even, b_, rows) + tail))
    for n in range(3):
        for states, b_, t_ in ((od_p, bp, t), (od_s, bs, ts)):
            tail = (c_heads,) if n == 2 else (c_heads, HEAD_DIM)
            outs.append(jnp.stack([s[n] for s in states]).reshape((len(states), b_, t_) + tail))
    return tuple(outs)
```

```python
import functools

import numpy as np
import jax
import jax.numpy as jnp
from jax import lax
from jax.experimental import pallas as pl
from jax.experimental.pallas import tpu as pltpu

F32 = jnp.float32
BF16 = jnp.bfloat16

HEAD_DIM = 64
A_KV = 2
IDX_HEADS = 4
IDX_DIM = 64
DSA_TOPK = 256
B_KV = 2
CMP_LEN = 32
CMP_STRIDE = 16
SEL_LEN = 64
SEL_TOP = 16
WINDOW = 512
ROPE_THETA = 10000.0
RMS_EPS = 1e-6
NEG_BIG = -1e30
FORCE_SCORE = 1e9
PAGE_SIZE = 128

LANES = 128
BF16_SUBLANES = 16
VMEM_LIMIT = 56 * 1024 * 1024

GATHER_PAGES = 4
KEY_CHUNK = GATHER_PAGES * PAGE_SIZE
WIN_CHUNK = 256
ATT_UNROLL = 2
TQ_PROMPT = 512
TQ_PROMPT_NSA = 256
TQ_SAMPLE = BF16_SUBLANES
TM_ROWS = 256
FF_CHUNK = 1024

PLAIN, ROPE, NORM_ROPE, NORM, LOGSIG = range(5)

INT_MIN = np.int32(-2 ** 31)
KEY_NINF = np.int32(-2139095041)


def _dot_nt(a, b):
    return lax.dot_general(a, b, (((1,), (1,)), ((), ())), preferred_element_type=F32)


def _iota(shape, dim):
    return lax.broadcasted_iota(jnp.int32, shape, dim)


def _round_up(x, m):
    return -(-x // m) * m


def _div_pow2(x, n):
    assert n & (n - 1) == 0
    if isinstance(x, int):
        return x // n
    return lax.shift_right_arithmetic(x, jnp.int32(n.bit_length() - 1))


def _loop(lo, hi, body, init, static, unroll=1):
    if static:
        carry = init
        for c in range(lo, hi):
            carry = body(c, carry)
        return carry
    if unroll == 1:
        return lax.fori_loop(lo, hi, body, init)

    def trip(i, carry):
        for u in range(unroll):
            carry = body(i * unroll + u, carry)
        return carry

    return lax.fori_loop(_div_pow2(lo, unroll), _div_pow2(hi, unroll), trip, init)


def _round_up_pow2(x, n):
    return _div_pow2(x + (n - 1), n) * n


def _min(a, b):
    return min(a, b) if isinstance(a, int) and isinstance(b, int) else jnp.minimum(a, b)


def _max(a, b):
    return max(a, b) if isinstance(a, int) and isinstance(b, int) else jnp.maximum(a, b)


def _chunk_start(c, ch):
    return c * ch if isinstance(c, int) else pl.multiple_of(c * ch, ch)


def _inproj_body(x_ref, g_ref, w_ref, gains_ref, cos_ref, sin_ref, bias_ref, o32_ref, o16_ref, *, kinds):
    x = x_ref[...]
    xn = x * lax.rsqrt(jnp.mean(x * x, axis=-1, keepdims=True) + RMS_EPS) * g_ref[...]
    xb = xn.astype(BF16)
    cos = cos_ref[...]
    sin = sin_ref[...]
    lane = _iota((1, LANES), 1)
    lo = lane < HEAD_DIM
    first_half = (lane & (HEAD_DIM - 1)) < (HEAD_DIM // 2)
    n_chunks = len(kinds)
    group = 4
    for c0 in range(0, n_chunks, group):
        c1 = min(c0 + group, n_chunks)
        hh = jnp.dot(xb, w_ref[:, c0 * LANES:c1 * LANES], preferred_element_type=F32)
        for c in range(c0, c1):
            h = hh[:, (c - c0) * LANES:(c - c0 + 1) * LANES]
            kind, gi = kinds[c]
            if kind in (NORM_ROPE, NORM):
                ss = h * h
                s_lo = jnp.sum(jnp.where(lo, ss, 0.0), axis=-1, keepdims=True)
                s_hi = jnp.sum(jnp.where(lo, 0.0, ss), axis=-1, keepdims=True)
                ms = jnp.where(lo, s_lo, s_hi) * (1.0 / HEAD_DIM)
                h = h * lax.rsqrt(ms + RMS_EPS) * gains_ref[gi:gi + 1, :]
            if kind in (NORM_ROPE, ROPE):
                partner = jnp.where(first_half, pltpu.roll(h, LANES - HEAD_DIM // 2, 1),
                                    pltpu.roll(h, HEAD_DIM // 2, 1))
                h = h * cos + partner * sin
            if kind == LOGSIG:
                z = h + bias_ref[...]
                h = -(jnp.maximum(-z, 0.0) + jnp.log1p(jnp.exp(-jnp.abs(z))))
            o32_ref[:, c * LANES:(c + 1) * LANES] = h
            o16_ref[:, c * LANES:(c + 1) * LANES] = h.astype(BF16)


def _inproj(x, g, w, gains, cos, sin, bias, kinds, tm):
    n, d = x.shape
    nout = w.shape[1]
    n_tab = cos.shape[0] // tm
    const = lambda i: (0, 0)
    return pl.pallas_call(
        functools.partial(_inproj_body, kinds=kinds),
        grid=(n // tm,),
        in_specs=[
            pl.BlockSpec((tm, d), lambda i: (i, 0)),
            pl.BlockSpec((1, d), const),
            pl.BlockSpec((d, nout), const, pipeline_mode=pl.Buffered(1)),
            pl.BlockSpec(gains.shape, const),
            pl.BlockSpec((tm, LANES), lambda i: (i % n_tab, 0)),
            pl.BlockSpec((tm, LANES), lambda i: (i % n_tab, 0)),
            pl.BlockSpec((1, LANES), const),
        ],
        out_specs=[pl.BlockSpec((tm, nout), lambda i: (i, 0)),
                   pl.BlockSpec((tm, nout), lambda i: (i, 0))],
        out_shape=[jax.ShapeDtypeStruct((n, nout), F32), jax.ShapeDtypeStruct((n, nout), BF16)],
        compiler_params=pltpu.CompilerParams(dimension_semantics=("parallel",), vmem_limit_bytes=VMEM_LIMIT),
        name="inproj",
    )(x, g.reshape(1, d), w, gains, cos, sin, bias)


def _postmix_body(*refs, n_o):
    x_ref = refs[0]
    o_refs = refs[1:1 + n_o]
    w_ref, g_ref, wup_ref, wdn_ref, out_ref = refs[1 + n_o:]
    o = o_refs[0][...] if n_o == 1 else jnp.concatenate([r[...] for r in o_refs], axis=-1)
    x1 = x_ref[...] + jnp.dot(o, w_ref[...], preferred_element_type=F32)
    xn = x1 * lax.rsqrt(jnp.mean(x1 * x1, axis=-1, keepdims=True) + RMS_EPS) * g_ref[...]
    xb = xn.astype(BF16)
    acc = x1
    d_ff = wup_ref.shape[1]
    for c in range(d_ff // FF_CHUNK):
        h = jnp.dot(xb, wup_ref[:, c * FF_CHUNK:(c + 1) * FF_CHUNK], preferred_element_type=F32)
        a = jnp.square(jnp.maximum(h, 0.0)).astype(BF16)
        acc = acc + jnp.dot(a, wdn_ref[c * FF_CHUNK:(c + 1) * FF_CHUNK, :], preferred_element_type=F32)
    out_ref[...] = acc


def _postmix(x, os_, ws, g, w_up, w_down, tm):
    n, d = x.shape
    const = lambda i: (0, 0)
    row = lambda i: (i, 0)
    w_out = ws[0] if len(ws) == 1 else jnp.concatenate(ws, axis=0)
    in_specs = [pl.BlockSpec((tm, d), row)]
    in_specs += [pl.BlockSpec((tm, o.shape[1]), row) for o in os_]
    in_specs += [pl.BlockSpec(w_out.shape, const, pipeline_mode=pl.Buffered(1)),
                 pl.BlockSpec((1, d), const),
                 pl.BlockSpec(w_up.shape, const, pipeline_mode=pl.Buffered(1)),
                 pl.BlockSpec(w_down.shape, const, pipeline_mode=pl.Buffered(1))]
    return pl.pallas_call(
        functools.partial(_postmix_body, n_o=len(os_)),
        grid=(n // tm,),
        in_specs=in_specs,
        out_specs=pl.BlockSpec((tm, d), row),
        out_shape=jax.ShapeDtypeStruct((n, d), F32),
        compiler_params=pltpu.CompilerParams(dimension_semantics=("parallel",), vmem_limit_bytes=VMEM_LIMIT),
        name="postmix",
    )(x, *os_, w_out, g.reshape(1, d), w_up, w_down)


def _gather_body(pt_ref, *refs, n, pps, n_steps, dup, has_new):
    del pt_ref
    page_refs = refs[:n * pps]
    new_refs = refs[n * pps:n * pps + (n if has_new else 0)]
    out_refs = refs[n * pps + len(new_refs):]
    rows = page_refs[0].shape[0]

    def widen(val, a):
        return jnp.concatenate([val, val], axis=-1) if dup[a] else val

    def copy_pages():
        for a in range(n):
            for u in range(pps):
                out_refs[a][u * rows:(u + 1) * rows, :] = widen(page_refs[a * pps + u][...], a).astype(out_refs[a].dtype)

    if not has_new:
        copy_pages()
        return
    p = pl.program_id(1)
    pl.when(p < n_steps)(copy_pages)

    @pl.when(p == n_steps)
    def _():
        for a in range(n):
            new = widen(new_refs[a][...], a)
            r = new.shape[0]
            out_refs[a][0:r, :] = new.astype(out_refs[a].dtype)
            out_refs[a][r:, :] = jnp.zeros((pps * rows - r, new.shape[1]), out_refs[a].dtype)


def _gather(page_table, caches, layer, news, dup, out_dtypes, pps):
    bs, n_pages = page_table.shape
    assert n_pages % pps == 0
    n_steps = n_pages // pps
    n = len(caches)
    has_new = news is not None
    rows = caches[0].shape[2]
    in_specs, out_specs, out_shape = [], [], []
    for c in caches:
        for u in range(pps):
            in_specs.append(pl.BlockSpec(
                (None, None, rows, c.shape[-1]),
                lambda b, p, pt, u=u: (layer, pt[b, jnp.minimum(p, n_steps - 1) * pps + u], 0, 0)))
    if has_new:
        for nw in news:
            in_specs.append(pl.BlockSpec((None,) + nw.shape[1:], lambda b, p, pt: (b, 0, 0)))
    for c, d_, dt in zip(caches, dup, out_dtypes):
        width = c.shape[-1] * (2 if d_ else 1)
        out_specs.append(pl.BlockSpec((None, pps * rows, width), lambda b, p, pt: (b, p, 0)))
        out_shape.append(jax.ShapeDtypeStruct((bs, (n_steps + has_new) * pps * rows, width), dt))
    args = [a for c in caches for a in [c] * pps] + (list(news) if has_new else [])
    return pl.pallas_call(
        functools.partial(_gather_body, n=n, pps=pps, n_steps=n_steps, dup=tuple(dup), has_new=has_new),
        grid_spec=pltpu.PrefetchScalarGridSpec(
            num_scalar_prefetch=1, grid=(bs, n_steps + has_new), in_specs=in_specs, out_specs=out_specs),
        out_shape=out_shape,
        compiler_params=pltpu.CompilerParams(dimension_semantics=("parallel", "arbitrary"),
                                             vmem_limit_bytes=VMEM_LIMIT),
        name="paged_gather",
    )(page_table, *args)


def _gather_t_body(pt_ref, *refs, n, pps, n_steps, dup):
    del pt_ref
    page_refs = refs[:n * pps]
    new_refs = refs[n * pps:n * pps + n]
    out_refs = refs[n * pps + n:]

    def widen(val, a):
        return jnp.concatenate([val, val], axis=0) if dup[a] else val

    p = pl.program_id(1)

    @pl.when(p < n_steps)
    def _():
        for a in range(n):
            for u in range(pps):
                out_refs[a][:, u * PAGE_SIZE:(u + 1) * PAGE_SIZE] = widen(page_refs[a * pps + u][...], a).astype(
                    out_refs[a].dtype)

    @pl.when(p == n_steps)
    def _():
        for a in range(n):
            new = widen(new_refs[a][...], a)
            r = new.shape[1]
            out_refs[a][:, 0:r] = new.astype(out_refs[a].dtype)
            out_refs[a][:, r:] = jnp.zeros((new.shape[0], pps * PAGE_SIZE - r), out_refs[a].dtype)


def _gather_t(page_table, caches, layer, news, dup, out_dtypes, pps):
    bs, n_pages = page_table.shape
    assert n_pages % pps == 0
    n_steps = n_pages // pps
    n = len(caches)
    in_specs, out_specs, out_shape = [], [], []
    for c in caches:
        for u in range(pps):
            in_specs.append(pl.BlockSpec(
                (None, None, c.shape[2], PAGE_SIZE),
                lambda b, p, pt, u=u: (layer, pt[b, jnp.minimum(p, n_steps - 1) * pps + u], 0, 0)))
    for nw in news:
        in_specs.append(pl.BlockSpec((None,) + nw.shape[1:], lambda b, p, pt: (b, 0, 0)))
    for c, d_, dt in zip(caches, dup, out_dtypes):
        feat = c.shape[2] * (2 if d_ else 1)
        out_specs.append(pl.BlockSpec((None, feat, pps * PAGE_SIZE), lambda b, p, pt: (b, 0, p)))
        out_shape.append(jax.ShapeDtypeStruct((bs, feat, (n_steps + 1) * pps * PAGE_SIZE), dt))
    args = [a for c in caches for a in [c] * pps] + list(news)
    return pl.pallas_call(
        functools.partial(_gather_t_body, n=n, pps=pps, n_steps=n_steps, dup=tuple(dup)),
        grid_spec=pltpu.PrefetchScalarGridSpec(
            num_scalar_prefetch=1, grid=(bs, n_steps + 1), in_specs=in_specs, out_specs=out_specs),
        out_shape=out_shape,
        compiler_params=pltpu.CompilerParams(dimension_semantics=("parallel", "arbitrary"),
                                             vmem_limit_bytes=VMEM_LIMIT),
        name="paged_gather_t",
    )(page_table, *args)


def _float_key(s):
    b = lax.bitcast_convert_type(s + 0.0, jnp.int32)
    return jnp.where(b >= 0, b, b ^ jnp.int32(0x7FFFFFFF))


def _prefix_tri():
    return jnp.where(_iota((LANES, LANES), 0) <= _iota((LANES, LANES), 1), 1.0, 0.0).astype(BF16)


def _topk_mask(key_ref, sel_ref, k, tri, n_used, static, additive, work=None):
    _, rows, width = key_ref.shape
    pieces = range(width // LANES)
    on, off = (0.0, NEG_BIG) if additive else (1.0, 0.0)
    kf = float(k)

    if static:
        def count(pred):
            def body(c, acc):
                for u in pieces:
                    acc = acc + jnp.where(pred(key_ref[c, :, u * LANES:(u + 1) * LANES]), 1.0, 0.0)
                return acc
            return jnp.sum(_loop(0, n_used, body, jnp.zeros((rows, LANES), F32), True), axis=-1, keepdims=True)

        zero = jnp.zeros((rows, 1), jnp.int32)
        t = jnp.where(count(lambda kc: kc >= zero) >= kf, zero, jnp.full((rows, 1), INT_MIN, jnp.int32))

        def bit_step(i, t):
            cand = t | jnp.left_shift(jnp.int32(1), 30 - i)
            return jnp.where(count(lambda kc: kc >= cand) >= kf, cand, t)

        t = lax.fori_loop(0, 31, bit_step, t)
        need = kf - count(lambda kc: kc > t)
    else:
        thr_ref, cnt_ref = work

        def count(strict):
            cnt_ref[...] = jnp.zeros((rows, LANES), F32)

            def body(c, _):
                cand = thr_ref[1]
                part = jnp.zeros((rows, LANES), F32)
                for u in pieces:
                    kc = key_ref[c, :, u * LANES:(u + 1) * LANES]
                    part = part + jnp.where((kc > cand) if strict else (kc >= cand), 1.0, 0.0)
                cnt_ref[...] += part
                return 0

            lax.fori_loop(0, n_used, body, 0)
            return jnp.sum(cnt_ref[...], axis=-1, keepdims=True)

        thr_ref[1] = jnp.zeros((rows, LANES), jnp.int32)
        thr_ref[0] = jnp.where(count(False) >= kf, thr_ref[1], jnp.full((rows, LANES), INT_MIN, jnp.int32))
        for i in range(31):
            thr_ref[1] = thr_ref[0] | jnp.int32(1 << (30 - i))
            thr_ref[0] = jnp.where(count(False) >= kf, thr_ref[1], thr_ref[0])
        thr_ref[1] = thr_ref[0]
        need = kf - count(True)
        t = thr_ref[0][:, 0:1]

    def mark(c, carry):
        for u in pieces:
            kc = key_ref[c, :, u * LANES:(u + 1) * LANES]
            eq = kc == t
            pc = jnp.dot(jnp.where(eq, 1.0, 0.0).astype(BF16), tri, preferred_element_type=F32) + carry
            sel = ((kc > t) | (eq & (pc <= need))) & (kc > KEY_NINF)
            sel_ref[c, :, u * LANES:(u + 1) * LANES] = jnp.where(sel, on, off)
            carry = jnp.broadcast_to(pc[:, LANES - 1:LANES], (rows, LANES))
        return carry

    _loop(0, n_used, mark, jnp.zeros((rows, LANES), F32), static)


def _key_chunk(ref, c, ch, kv_t):
    keys = pl.ds(_chunk_start(c, ch), ch)
    return ref[0, :, keys] if kv_t else ref[0, keys, :]


def _qk(q, kc, kv_t):
    return jnp.dot(q, kc, preferred_element_type=F32) if kv_t else _dot_nt(q, kc)


def _attend_pair(q2, k_ref, v_ref, s_ref, bounds, ch, mask_fn, static, unroll=1, kidx=None, kv_t=False):
    c_lo, c_mid, c_hi = bounds
    tq = q2.shape[0]
    pieces = range(ch // LANES)
    lo = _iota((1, LANES), 1) < HEAD_DIM
    zero = jnp.zeros_like(q2)
    qs = (jnp.where(lo, q2, zero), jnp.where(lo, zero, q2))
    chunk_of = lambda ref, c: _key_chunk(ref, c if kidx is None else kidx(c), ch, kv_t)

    def score(c, mvecs, diag):
        kc = chunk_of(k_ref, c)
        out = []
        for half in range(2):
            s = _qk(qs[half], kc, kv_t)
            extra = mask_fn(c, half, diag)
            if extra is not None:
                s = s + extra
            s_ref[half, c] = s
            mvec = mvecs[half]
            for u in pieces:
                mvec = jnp.maximum(mvec, s[:, u * LANES:(u + 1) * LANES])
            out.append(mvec)
        return tuple(out)

    mvecs = (jnp.full((tq, LANES), NEG_BIG, F32),) * 2
    mvecs = _loop(c_lo, c_mid, functools.partial(score, diag=False), mvecs, static, unroll)
    mvecs = _loop(c_mid, c_hi, functools.partial(score, diag=True), mvecs, static, unroll)
    ms = [jnp.max(mv, axis=-1, keepdims=True) for mv in mvecs]

    def accumulate(c, carry):
        vc = chunk_of(v_ref, c)
        out = []
        for half in range(2):
            lvec, acc = carry[2 * half:2 * half + 2]
            p = jnp.exp(s_ref[half, c] - ms[half])
            for u in pieces:
                lvec = lvec + p[:, u * LANES:(u + 1) * LANES]
            pv = _dot_nt(p.astype(BF16), vc) if kv_t else jnp.dot(p.astype(BF16), vc, preferred_element_type=F32)
            out += [lvec, acc + pv]
        return tuple(out)

    res = _loop(c_lo, c_hi, accumulate, (jnp.zeros((tq, LANES), F32),) * 4, static, unroll)
    outs = []
    for half in range(2):
        l = jnp.sum(res[2 * half], axis=-1, keepdims=True)
        outs.append(jnp.where(ms[half] > 0.5 * NEG_BIG, res[2 * half + 1] / jnp.maximum(l, 1e-30), 0.0))
    return jnp.where(lo, outs[0], outs[1])


def _scaled(q, scale):
    return (q.astype(F32) * scale).astype(BF16)


def _query_block(qpos0, tq, axis, static):
    q0 = qpos0 if static else qpos0 + pl.program_id(axis) * tq
    return q0, q0 + _iota((tq, 1), 0)


def _causal_chunks(q0, tq, n_ch, ch):
    c_hi = _min(n_ch, _div_pow2(q0 + tq - 1, ch) + 1)
    c_mid = _min(c_hi, _div_pow2(q0 + 1, ch))
    if isinstance(c_hi, int):
        return c_mid, c_hi
    assert n_ch % ATT_UNROLL == 0
    return _div_pow2(c_mid, ATT_UNROLL) * ATT_UNROLL, _round_up_pow2(c_hi, ATT_UNROLL)


def _dsa_body(qa_ref, iq_ref, misc_ref, k_ref, v_ref, ik_ref, o_ref, key_ref, sel_ref, s_ref, thr_ref, cnt_ref,
              *, tq, ch, topk, qpos0, n_keys, static, kv_t):
    n_ch = key_ref.shape[0]
    q0, qpos = _query_block(qpos0, tq, 1, static)
    _, c_hi = _causal_chunks(q0, tq, n_ch, ch)
    lo = _iota((1, LANES), 1) < HEAD_DIM
    iw = misc_ref[0][:, 0:IDX_HEADS]
    tri = _prefix_tri()

    def score_body(c, _):
        ikc = _key_chunk(ik_ref, c, ch, kv_t)
        acc = jnp.zeros((tq, ch), F32)
        for hp in range(IDX_HEADS // 2):
            iqc = iq_ref[0, :, hp * LANES:(hp + 1) * LANES]
            zero = jnp.zeros_like(iqc)
            for half in range(2):
                qm = jnp.where(lo, iqc, zero) if half == 0 else jnp.where(lo, zero, iqc)
                s = _qk(qm, ikc, kv_t) * (IDX_DIM ** -0.5)
                h = 2 * hp + half
                acc = acc + jnp.maximum(s, 0.0) * iw[:, h:h + 1]
        sc = acc * (IDX_HEADS ** -0.5)
        kpos = c * ch + _iota((1, ch), 1)
        key_ref[c] = jnp.where((kpos <= qpos) & (kpos < n_keys), _float_key(sc), KEY_NINF)
        return 0

    _loop(0, c_hi, score_body, 0, static, ATT_UNROLL)
    _topk_mask(key_ref, sel_ref, topk, tri, c_hi, static, additive=True, work=(thr_ref, cnt_ref))

    for j in range(qa_ref.shape[2] // LANES):
        q2 = _scaled(qa_ref[0, :, j * LANES:(j + 1) * LANES], HEAD_DIM ** -0.5)
        o = _attend_pair(q2, k_ref, v_ref, s_ref, (0, 0, c_hi), ch, lambda c, half, diag: sel_ref[c],
                         static, ATT_UNROLL, kv_t=kv_t)
        o_ref[0, :, j * LANES:(j + 1) * LANES] = o.astype(o_ref.dtype)


def _kv_len(arr, kv_t):
    return arr.shape[2] if kv_t else arr.shape[1]


def _kv_spec(operand, kv_t, chunk_of_step=lambda *g: 0):
    arr, c0 = operand
    s = _kv_len(arr, kv_t)
    if kv_t:
        return pl.BlockSpec((1, LANES, s), lambda bi, *g: (bi, c0 + chunk_of_step(*g), 0))
    return pl.BlockSpec((1, s, LANES), lambda bi, *g: (bi, 0, c0 + chunk_of_step(*g)))


def _dsa(q16, m32, src, col, tq, topk, qpos0, n_keys, static, kv_t):
    b, t, _ = q16.shape
    s = _kv_len(src["k"][0], kv_t)
    ch = KEY_CHUNK
    n_qa = 4 * LANES
    kv_spec = lambda name: _kv_spec(src[name], kv_t)
    return pl.pallas_call(
        functools.partial(_dsa_body, tq=tq, ch=ch, topk=topk, qpos0=qpos0, n_keys=n_keys, static=static,
                          kv_t=kv_t),
        grid=(b, t // tq),
        in_specs=[
            pl.BlockSpec((1, tq, n_qa), lambda bi, i: (bi, i, col["qa"] * LANES // n_qa)),
            pl.BlockSpec((1, tq, 2 * LANES), lambda bi, i: (bi, i, col["iq"] // 2)),
            pl.BlockSpec((1, tq, LANES), lambda bi, i: (bi, i, col["misc"])),
            kv_spec("k"), kv_spec("v"), kv_spec("ik"),
        ],
        out_specs=pl.BlockSpec((1, tq, n_qa), lambda bi, i: (bi, i, 0)),
        out_shape=jax.ShapeDtypeStruct((b, t, n_qa), BF16),
        scratch_shapes=[pltpu.VMEM((s // ch, tq, ch), jnp.int32), pltpu.VMEM((s // ch, tq, ch), F32),
                        pltpu.VMEM((2, s // ch, tq, ch), F32),
                        pltpu.VMEM((2, tq, LANES), jnp.int32), pltpu.VMEM((tq, LANES), F32)],
        compiler_params=pltpu.CompilerParams(dimension_semantics=("parallel", "arbitrary"),
                                             vmem_limit_bytes=VMEM_LIMIT),
        name="dsa",
    )(q16, q16, m32, src["k"][0], src["v"][0], src["ik"][0])


def _nsa_body(qb_ref, misc_ref, cmpk_ref, cmpv_ref, sk_ref, sv_ref, wk_ref, wv_ref, imp_ref, exp_ref,
              o_ref, key_ref, selb_ref, msk_ref, s_ref, sw_ref,
              *, tq, ch, n_sel, qpos0, wpos0, n_wvalid, static, kv_t):
    n_ch = exp_ref.shape[0]
    ncp = cmpk_ref.shape[1]
    nsp = imp_ref.shape[1]
    q0, qpos = _query_block(qpos0, tq, 1, static)
    _, c_hi = _causal_chunks(q0, tq, n_ch, ch)
    lo = _iota((1, LANES), 1) < HEAD_DIM
    scale = HEAD_DIM ** -0.5
    tri = _prefix_tri()
    n_chunks_q = qb_ref.shape[2] // LANES
    gates = jax.nn.sigmoid(misc_ref[0])

    cvalid = (_iota((1, ncp), 1) * CMP_STRIDE + (CMP_LEN - 1)) <= qpos
    ck = cmpk_ref[0]
    cv = cmpv_ref[0]
    imp_c = [jnp.zeros((tq, ncp), F32), jnp.zeros((tq, ncp), F32)]
    o_cmp = []
    for j in range(n_chunks_q):
        q2 = qb_ref[0, :, j * LANES:(j + 1) * LANES]
        zero = jnp.zeros_like(q2)
        outs = []
        for half in range(2):
            qm = jnp.where(lo, q2, zero) if half == 0 else jnp.where(lo, zero, q2)
            s = jnp.where(cvalid, _dot_nt(qm, ck) * scale, NEG_BIG)
            p = jnp.where(cvalid, jnp.exp(s - jnp.max(s, axis=-1, keepdims=True)), 0.0)
            pn = p / jnp.maximum(jnp.sum(p, axis=-1, keepdims=True), 1e-30)
            imp_c[half] = imp_c[half] + pn
            outs.append(jnp.dot(pn.astype(BF16), cv, preferred_element_type=F32))
        o_cmp.append(jnp.where(lo, outs[0], outs[1]))

    blk = _iota((1, nsp), 1)
    adm = blk * SEL_LEN <= qpos
    cur = _div_pow2(qpos, SEL_LEN)
    forced = adm & ((blk == 0) | (blk == cur) | (blk == cur - 1))
    for g in range(2):
        imp = jnp.dot(imp_c[g], imp_ref[...], preferred_element_type=F32, precision=lax.Precision.HIGHEST)
        score = jnp.where(forced, FORCE_SCORE, jnp.where(adm, imp, -jnp.inf))
        key_ref[0, g * tq:(g + 1) * tq, :] = _float_key(score)
    _topk_mask(key_ref, selb_ref, n_sel, tri, 1, True, additive=False)
    for g in range(2):
        bm = selb_ref[0, g * tq:(g + 1) * tq, :].astype(BF16)

        def expand(c, _):
            kpos = c * ch + _iota((1, ch), 1)
            hit = (jnp.dot(bm, exp_ref[c], preferred_element_type=F32) > 0.5) & (kpos <= qpos)
            msk_ref[g, c] = jnp.where(hit, 0.0, NEG_BIG)
            return 0

        _loop(0, c_hi, expand, 0, static, ATT_UNROLL)

    wch = WIN_CHUNK
    n_wch = wk_ref.shape[2 if kv_t else 1] // wch
    n_win = sw_ref.shape[1]
    wc0 = _div_pow2(q0 - WINDOW - wpos0, wch)

    def win_chunk(u):
        c = wc0 + u
        return _min(_max(c, 0), n_wch - 1)

    def win_mask(u, half, diag):
        row = (wc0 + u) * wch + _iota((1, wch), 1)
        kpos = wpos0 + row
        dlt = qpos - kpos
        ok = (dlt >= 0) & (dlt <= WINDOW) & (kpos >= 0) & (row >= 0) & (row < n_wvalid)
        return jnp.where(ok, 0.0, NEG_BIG)

    for j in range(n_chunks_q):
        q2 = _scaled(qb_ref[0, :, j * LANES:(j + 1) * LANES], scale)
        o_sel = _attend_pair(q2, sk_ref, sv_ref, s_ref, (0, 0, c_hi), ch, lambda c, half, diag: msk_ref[half, c],
                             static, ATT_UNROLL, kv_t=kv_t)
        o_win = _attend_pair(q2, wk_ref, wv_ref, sw_ref, (0, 0, n_win), wch, win_mask, True, kidx=win_chunk,
                             kv_t=kv_t)
        o = jnp.zeros((tq, LANES), F32)
        for bi, ob in enumerate((o_cmp[j], o_sel, o_win)):
            c_a = IDX_HEADS + 3 * j + bi
            c_b = IDX_HEADS + 3 * (n_chunks_q + j) + bi
            o = o + jnp.where(lo, gates[:, c_a:c_a + 1], gates[:, c_b:c_b + 1]) * ob
        o_ref[0, :, j * LANES:(j + 1) * LANES] = o.astype(o_ref.dtype)


def _nsa(q16, m32, cmpk, cmpv, src, imp_mat, exp_mat, col, tq, n_sel, qpos0, wpos0, n_wvalid, static, kv_t):
    b, t, _ = q16.shape
    s = _kv_len(src["sk"][0], kv_t)
    ch = exp_mat.shape[2]
    ncp, nsp = imp_mat.shape
    n_qb = 4 * LANES
    assert t == tq or tq % WIN_CHUNK == 0
    mis = (qpos0 - WINDOW - wpos0) % WIN_CHUNK
    n_win = (mis + WINDOW + tq - 1) // WIN_CHUNK + 1
    spec = lambda name: _kv_spec(src[name], kv_t)
    return pl.pallas_call(
        functools.partial(_nsa_body, tq=tq, ch=ch, n_sel=n_sel, qpos0=qpos0, wpos0=wpos0, n_wvalid=n_wvalid,
                          static=static, kv_t=kv_t),
        grid=(b, t // tq),
        in_specs=[
            pl.BlockSpec((1, tq, n_qb), lambda bi, i: (bi, i, col["qb"] * LANES // n_qb)),
            pl.BlockSpec((1, tq, LANES), lambda bi, i: (bi, i, col["misc"])),
            pl.BlockSpec((1, ncp, LANES), lambda bi, i: (bi, 0, 0)),
            pl.BlockSpec((1, ncp, LANES), lambda bi, i: (bi, 0, 0)),
            spec("sk"), spec("sv"), spec("wk"), spec("wv"),
            pl.BlockSpec((ncp, nsp), lambda bi, i: (0, 0)),
            pl.BlockSpec(exp_mat.shape, lambda bi, i: (0, 0, 0)),
        ],
        out_specs=pl.BlockSpec((1, tq, n_qb), lambda bi, i: (bi, i, 0)),
        out_shape=jax.ShapeDtypeStruct((b, t, n_qb), BF16),
        scratch_shapes=[pltpu.VMEM((1, 2 * tq, nsp), jnp.int32), pltpu.VMEM((1, 2 * tq, nsp), F32),
                        pltpu.VMEM((2, s // ch, tq, ch), F32), pltpu.VMEM((2, s // ch, tq, ch), F32),
                        pltpu.VMEM((2, n_win, tq, WIN_CHUNK), F32)],
        compiler_params=pltpu.CompilerParams(dimension_semantics=("parallel", "arbitrary"),
                                             vmem_limit_bytes=VMEM_LIMIT),
        name="nsa",
    )(q16, m32, cmpk, cmpv, src["sk"][0], src["sv"][0], src["wk"][0], src["wv"][0], imp_mat, exp_mat)


def _compress_body(uk_ref, uv_ref, wk_ref, wv_ref, pe_ref, ok_ref, ov_ref, shift_ref):
    nu = uk_ref.shape[1]
    half = wk_ref.shape[0] // 2
    shift_ref[nu:nu + 8, :] = jnp.zeros((8, LANES), F32)
    for u_ref, w_ref, o_ref in ((uk_ref, wk_ref, ok_ref), (uv_ref, wv_ref, ov_ref)):
        u = u_ref[0]
        top = jnp.dot(u, w_ref[0:half, :], preferred_element_type=F32)
        shift_ref[0:nu, :] = jnp.dot(u, w_ref[half:2 * half, :], preferred_element_type=F32)
        pec = jnp.dot(pe_ref[...], w_ref[...], preferred_element_type=F32)[0:1, :]
        o_ref[0] = (top + shift_ref[1:nu + 1, :] + pec).astype(o_ref.dtype)


def _compress(uk, uv, wk2, wv2, pe2):
    b, nu, width = uk.shape
    const = lambda bi: (0, 0)
    row = lambda bi: (bi, 0, 0)
    return pl.pallas_call(
        _compress_body,
        grid=(b,),
        in_specs=[pl.BlockSpec((1, nu, width), row), pl.BlockSpec((1, nu, width), row),
                  pl.BlockSpec(wk2.shape, const), pl.BlockSpec(wv2.shape, const), pl.BlockSpec(pe2.shape, const)],
        out_specs=[pl.BlockSpec((1, nu, LANES), row), pl.BlockSpec((1, nu, LANES), row)],
        out_shape=[jax.ShapeDtypeStruct((b, nu, LANES), BF16)] * 2,
        scratch_shapes=[pltpu.VMEM((nu + 8, LANES), F32)],
        compiler_params=pltpu.CompilerParams(dimension_semantics=("parallel",), vmem_limit_bytes=VMEM_LIMIT),
        name="compress",
    )(uk, uv, wk2, wv2, pe2)


def _cumsum_body(x_ref, o_ref):
    s = x_ref.shape[2]
    tri = jnp.where(_iota((LANES, LANES), 0) <= _iota((LANES, LANES), 1), 1.0, 0.0)
    carry = jnp.zeros((x_ref.shape[1], 1), F32)
    for c in range(s // LANES):
        blk = x_ref[0, :, c * LANES:(c + 1) * LANES]
        pc = jnp.dot(blk, tri, preferred_element_type=F32, precision=lax.Precision.HIGHEST) + carry
        o_ref[0, :, c * LANES:(c + 1) * LANES] = pc
        carry = pc[:, LANES - 1:LANES]


def _cumsum(x):
    b, h, s = x.shape
    return pl.pallas_call(
        _cumsum_body,
        grid=(b,),
        in_specs=[pl.BlockSpec((1, h, s), lambda bi: (bi, 0, 0))],
        out_specs=pl.BlockSpec((1, h, s), lambda bi: (bi, 0, 0)),
        out_shape=jax.ShapeDtypeStruct((b, h, s), F32),
        compiler_params=pltpu.CompilerParams(dimension_semantics=("parallel",)),
        name="cumsum",
    )(x)


def _fox_body(q_ref, k_ref, v_ref, cq_ref, ck_ref, o_ref, s_ref, *, tq, ch, qpos0, n_keys, static, kv_t):
    n_ch = ck_ref.shape[2]
    q0, qpos = _query_block(qpos0, tq, 2, static)
    c_mid, c_hi = _causal_chunks(q0, tq, n_ch, ch)
    cq = cq_ref[0, 0]

    def bias_fn(c, half, diag):
        bias = cq[:, half:half + 1] - ck_ref[0, 0, c, half:half + 1, :]
        if diag:
            kpos = c * ch + _iota((1, ch), 1)
            bias = jnp.where((kpos <= qpos) & (kpos < n_keys), bias, NEG_BIG)
        return bias

    q2 = _scaled(q_ref[0], HEAD_DIM ** -0.5)
    o = _attend_pair(q2, k_ref, v_ref, s_ref, (0, c_mid, c_hi), ch, bias_fn, static, ATT_UNROLL, kv_t=kv_t)
    o_ref[0] = o.astype(o_ref.dtype)


def _fox(q16, src, cq, ck, qcol, tq, qpos0, n_keys, static, kv_t):
    b, t, _ = q16.shape
    s = _kv_len(src["k"][0], kv_t)
    n_pairs = cq.shape[1]
    ch = ck.shape[4]
    kv_spec = lambda name: _kv_spec(src[name], kv_t, lambda hp, i: hp)
    return pl.pallas_call(
        functools.partial(_fox_body, tq=tq, ch=ch, qpos0=qpos0, n_keys=n_keys, static=static, kv_t=kv_t),
        grid=(b, n_pairs, t // tq),
        in_specs=[
            pl.BlockSpec((1, tq, LANES), lambda bi, hp, i: (bi, i, qcol + hp)),
            kv_spec("k"), kv_spec("v"),
            pl.BlockSpec((1, 1, tq, 2), lambda bi, hp, i: (bi, hp, i, 0)),
            pl.BlockSpec((1, 1, s // ch, 2, ch), lambda bi, hp, i: (bi, hp, 0, 0, 0)),
        ],
        out_specs=pl.BlockSpec((1, tq, LANES), lambda bi, hp, i: (bi, i, hp)),
        out_shape=jax.ShapeDtypeStruct((b, t, n_pairs * LANES), BF16),
        scratch_shapes=[pltpu.VMEM((2, s // ch, tq, ch), F32)],
        compiler_params=pltpu.CompilerParams(dimension_semantics=("parallel", "parallel", "arbitrary"),
                                             vmem_limit_bytes=VMEM_LIMIT),
        name="fox",
    )(q16, src["k"][0], src["v"][0], cq, ck)


E_QA, E_KA, E_VA, E_IQ, E_QB, E_IK, E_CK, E_CV, E_SK, E_SV, E_WK, E_WV, E_MISC = 0, 4, 5, 6, 8, 12, 13, 14, 15, 16, 17, 18, 19
G_QA, G_KA, G_QB, G_CMP, G_SEL, G_WIN = range(6)
E_KINDS = ((NORM_ROPE, G_QA),) * 4 + ((NORM_ROPE, G_KA), (PLAIN, 0)) + ((ROPE, 0),) * 2 + ((NORM_ROPE, G_QB),) * 4 + (
    (ROPE, 0), (NORM_ROPE, G_CMP), (PLAIN, 0), (NORM_ROPE, G_SEL), (PLAIN, 0), (NORM_ROPE, G_WIN), (PLAIN, 0), (PLAIN, 0))
O_Q, O_K, O_V, O_MISC = 0, 8, 16, 24
O_KINDS = ((NORM, 0),) * 8 + ((NORM, 1),) * 8 + ((PLAIN, 0),) * 8 + ((LOGSIG, 0),)


def _pair_heads(w, n_groups):
    d = w.shape[0]
    nj = w.shape[1] // (n_groups * HEAD_DIM)
    return w.reshape(d, n_groups, nj, HEAD_DIM).transpose(0, 2, 1, 3).reshape(d, -1)


def _even_weight(w_in):
    d = w_in.shape[0]
    sizes = (512, 128, 128, 256, 64, 4, 512, 128, 128, 128, 128, 128, 128, 24)
    offs = np.cumsum((0,) + sizes)
    qa, ka, va, iq, ik, iw, qb, ck, cv, sk, sv, wk, wv, gl = (w_in[:, offs[i]:offs[i + 1]] for i in range(14))
    misc = jnp.concatenate([iw, gl, jnp.zeros((d, LANES - 28), w_in.dtype)], axis=1)
    cols = [_pair_heads(qa, A_KV), ka, va, iq, _pair_heads(qb, B_KV), ik, ik, ck, cv, sk, sv, wk, wv, misc]
    return jnp.concatenate(cols, axis=1).astype(BF16)


def _odd_weight(w_in):
    d = w_in.shape[0]
    n = O_MISC * LANES
    fl = w_in[:, n:]
    return jnp.concatenate([w_in[:, :n], fl, jnp.zeros((d, LANES - fl.shape[1]), w_in.dtype)], axis=1).astype(BF16)


def _gain_table(gains):
    rows = [jnp.tile(g, 2) for g in gains]
    rows += [jnp.zeros((LANES,), F32)] * (8 - len(rows))
    return jnp.stack(rows)


def _rope_tables(pos):
    half = HEAD_DIM // 2
    inv = ROPE_THETA ** (-jnp.arange(half, dtype=F32) / half)
    ang = pos.astype(F32)[:, None] * inv[None, :]
    cos, sin = jnp.cos(ang), jnp.sin(ang)
    return jnp.tile(jnp.concatenate([cos, cos], axis=1), (1, 2)), jnp.tile(jnp.concatenate([-sin, sin], axis=1), (1, 2))


def _pair_rows(w_out_half, n_groups):
    nj = w_out_half.shape[0] // (n_groups * HEAD_DIM)
    return w_out_half.reshape(n_groups, nj, HEAD_DIM, -1).transpose(1, 0, 2, 3).reshape(w_out_half.shape)


def _compress_weight(w):
    w3 = w.reshape(CMP_LEN, HEAD_DIM, HEAD_DIM)
    eye = jnp.eye(B_KV, dtype=w.dtype)
    w5 = w3[:, None, :, None, :] * eye[None, :, None, :, None]
    return w5.reshape(CMP_LEN * B_KV * HEAD_DIM, B_KV * HEAD_DIM).astype(BF16)


def _importance_matrix(nc, ncp, ns, nsp):
    r, m = SEL_LEN // CMP_STRIDE, CMP_LEN // CMP_STRIDE
    mat = np.zeros((ncp, nsp), np.float32)
    for n in range(ns):
        for a in range(r):
            for b_ in range(m):
                c = n * r - m + 1 + a + b_
                if 0 <= c < nc:
                    mat[c, n] += 1.0
    return jnp.asarray(mat)


def _expand_matrix(nsp, s, ch):
    blk = np.arange(s) // SEL_LEN
    mat = (np.arange(nsp)[:, None] == blk[None, :]).astype(np.float32)
    return jnp.asarray(mat.reshape(nsp, s // ch, ch).transpose(1, 0, 2), dtype=BF16)


def _chunk(a, c, n=1):
    return a[..., c * LANES:(c + n) * LANES]


def _pad_rows(a, rows):
    return jnp.pad(a, ((0, 0), (0, rows - a.shape[1]), (0, 0)))


def kernel(x_prompt, x_sample, cache_a_k, cache_a_v, cache_a_idx, cache_b_cmp_k, cache_b_cmp_v, cache_b_sel_k, cache_b_sel_v, state_b_win_k, state_b_win_v, cache_c_k, cache_c_v, cache_c_logf, page_table, norm_mix, norm_ffn, e_w_in, e_qn_a, e_kn_a, e_qn_b, e_kn_cmp, e_kn_sel, e_kn_win, e_cmp_pos, e_w_cmp_k, e_w_cmp_v, e_w_out, o_w_in, o_b_f, o_qn, o_kn, o_w_out, w_up, w_down):
    bp, t, d = x_prompt.shape
    bs, ts, _ = x_sample.shape
    depth = norm_mix.shape[0]
    n_pages = page_table.shape[1]
    past = n_pages * PAGE_SIZE
    n_keys_s = past + ts
    s_s = past + KEY_CHUNK
    assert ts <= TQ_SAMPLE and t % KEY_CHUNK == 0 and past % KEY_CHUNK == 0
    n_buf = state_b_win_k.shape[2]
    c_heads = cache_c_k.shape[3]

    xp = x_prompt.reshape(bp * t, d)
    xs = x_sample.reshape(bs * ts, d)
    cos_p, sin_p = _rope_tables(jnp.arange(t))
    cos_s, sin_s = (jnp.tile(a, (bs, 1)) for a in _rope_tables(past + jnp.arange(ts)))
    zero_bias = jnp.zeros((1, LANES), F32)

    def flat_cache(c):
        return c.reshape(c.shape[:3] + (-1,))

    def feat_major(c):
        if c.ndim == 5:
            return c.transpose(0, 1, 3, 4, 2).reshape(c.shape[:2] + (c.shape[3] * c.shape[4], c.shape[2]))
        return c.transpose(0, 1, 3, 2)

    def key_major(a, heads):
        return a.reshape(a.shape[0], heads, a.shape[1] // heads, a.shape[2]).transpose(0, 3, 1, 2)

    def nsa_consts(n_keys, s):
        nc = (n_keys - CMP_LEN) // CMP_STRIDE + 1
        ncp = _round_up(nc, LANES)
        ns = -(-n_keys // SEL_LEN)
        nsp = _round_up(ns, LANES)
        return ncp, min(SEL_TOP, ns), _importance_matrix(nc, ncp, ns, nsp), _expand_matrix(nsp, s, KEY_CHUNK)

    def fit_rows(a, rows):
        return a[:, :rows] if a.shape[1] >= rows else _pad_rows(a, rows)

    ev_p, ev_s, od_p, od_s = [], [], [], []
    for li in range(depth):
        j = li // 2
        if li % 2 == 0:
            w = _even_weight(e_w_in[j])
            gains = _gain_table([e_qn_a[j], e_kn_a[j], e_qn_b[j], e_kn_cmp[j], e_kn_sel[j], e_kn_win[j]])
            hp32, hp16 = _inproj(xp, norm_mix[li], w, gains, cos_p, sin_p, zero_bias, E_KINDS, TM_ROWS)
            hs32, hs16 = _inproj(xs, norm_mix[li], w, gains, cos_s, sin_s, zero_bias, E_KINDS, bs * ts)
            hp32, hp16 = hp32.reshape(bp, t, -1), hp16.reshape(bp, t, -1)
            hs32, hs16 = hs32.reshape(bs, ts, -1), hs16.reshape(bs, ts, -1)
            wk2, wv2 = _compress_weight(e_w_cmp_k[j]), _compress_weight(e_w_cmp_v[j])
            pe2 = jnp.pad(jnp.tile(e_cmp_pos[j][:, None, :], (1, B_KV, 1)).reshape(1, -1),
                          ((0, BF16_SUBLANES - 1), (0, 0))).astype(BF16)
            cu = CMP_STRIDE * LANES
            col = dict(qa=E_QA, iq=E_IQ, misc=E_MISC, qb=E_QB)

            st_p = [_chunk(hp32, c) for c in (E_KA, E_VA)]
            st_p.append(_chunk(hp32, E_IK)[..., :IDX_DIM])
            st_p += [_chunk(hp32, c) for c in (E_CK, E_CV, E_SK, E_SV)]
            n_keep = min(WINDOW, t)
            st_p += [_chunk(hp32, c)[:, t - n_keep:] for c in (E_WK, E_WV)]
            ncp, n_sel, imp_mat, exp_mat = nsa_consts(t, t)
            cmpk, cmpv = _compress(_chunk(hp16, E_CK).reshape(bp, t // CMP_STRIDE, cu),
                                   _chunk(hp16, E_CV).reshape(bp, t // CMP_STRIDE, cu), wk2, wv2, pe2)
            src = {name: (hp16, c) for name, c in (("k", E_KA), ("v", E_VA), ("ik", E_IK), ("sk", E_SK),
                                                    ("sv", E_SV), ("wk", E_WK), ("wv", E_WV))}
            oa_p = _dsa(hp16, hp32, src, col, TQ_PROMPT, min(DSA_TOPK, t // 4), 0, t, False, False)
            ob_p = _nsa(hp16, hp32, fit_rows(cmpk, ncp), fit_rows(cmpv, ncp), src, imp_mat, exp_mat,
                        col, TQ_PROMPT_NSA, n_sel, 0, 0, t, False, False)

            st_s = [_chunk(hs32, c) for c in (E_KA, E_VA)]
            st_s.append(_chunk(hs32, E_IK)[..., :IDX_DIM])
            st_s += [_chunk(hs32, c) for c in (E_CK, E_CV, E_SK, E_SV)]
            wk_all = jnp.concatenate([feat_major(state_b_win_k)[j], jnp.swapaxes(_chunk(hs32, E_WK), 1, 2)], axis=2)
            wv_all = jnp.concatenate([feat_major(state_b_win_v)[j], jnp.swapaxes(_chunk(hs32, E_WV), 1, 2)], axis=2)
            n_keep = min(WINDOW, n_buf + ts)
            st_s += [key_major(a[:, :, n_buf + ts - n_keep:], B_KV) for a in (wk_all, wv_all)]
            caches = [feat_major(c) for c in (cache_a_k, cache_a_v, cache_a_idx, cache_b_sel_k, cache_b_sel_v)]
            news = [jnp.swapaxes(_pad_rows(st_s[n], TQ_SAMPLE), 1, 2) for n in (0, 1, 2, 5, 6)]
            gk, gv, gik, gsk, gsv = _gather_t(
                page_table, caches, j, news, (False, False, True, False, False), (BF16,) * 5, GATHER_PAGES)
            gck, gcv = _gather(page_table, [flat_cache(cache_b_cmp_k), flat_cache(cache_b_cmp_v)], j,
                               [_pad_rows(st_s[3], TQ_SAMPLE), _pad_rows(st_s[4], TQ_SAMPLE)],
                               (False, False), (BF16, BF16), GATHER_PAGES)
            qs16, qs32 = _pad_rows(hs16, TQ_SAMPLE), _pad_rows(hs32, TQ_SAMPLE)
            ncp, n_sel, imp_mat, exp_mat = nsa_consts(n_keys_s, s_s)
            cmpk, cmpv = _compress(gck.reshape(bs, s_s // CMP_STRIDE, cu), gcv.reshape(bs, s_s // CMP_STRIDE, cu),
                                   wk2, wv2, pe2)
            sw = _round_up(n_buf + ts, WIN_CHUNK)
            pad_keys = lambda a: jnp.pad(a, ((0, 0), (0, 0), (0, sw - a.shape[2]))).astype(BF16)
            src = dict(k=(gk, 0), v=(gv, 0), ik=(gik, 0), sk=(gsk, 0), sv=(gsv, 0),
                       wk=(pad_keys(wk_all), 0), wv=(pad_keys(wv_all), 0))
            oa_s = _dsa(qs16, qs32, src, col, TQ_SAMPLE, min(DSA_TOPK, n_keys_s // 4), past, n_keys_s, True, True)
            ob_s = _nsa(qs16, qs32, fit_rows(cmpk, ncp), fit_rows(cmpv, ncp), src, imp_mat, exp_mat,
                        col, TQ_SAMPLE, n_sel, past, past - n_buf, n_buf + ts, True, True)
            ev_p.append(st_p)
            ev_s.append(st_s)
            half = e_w_out.shape[1] // 2
            w_outs = [_pair_rows(e_w_out[j][:half], A_KV).astype(BF16), _pair_rows(e_w_out[j][half:], B_KV).astype(BF16)]
            os_p = [oa_p.reshape(bp * t, -1), ob_p.reshape(bp * t, -1)]
            os_s = [oa_s[:, :ts].reshape(bs * ts, -1), ob_s[:, :ts].reshape(bs * ts, -1)]
        else:
            w = _odd_weight(o_w_in[j])
            gains = _gain_table([o_qn[j], o_kn[j]])
            bias = jnp.pad(o_b_f[j], (0, LANES - c_heads)).reshape(1, LANES)
            hp32, hp16 = _inproj(xp, norm_mix[li], w, gains, cos_p, sin_p, bias, O_KINDS, TM_ROWS)
            hs32, hs16 = _inproj(xs, norm_mix[li], w, gains, cos_s, sin_s, bias, O_KINDS, bs * ts)
            hp32, hp16 = hp32.reshape(bp, t, -1), hp16.reshape(bp, t, -1)
            hs32, hs16 = hs32.reshape(bs, ts, -1), hs16.reshape(bs, ts, -1)
            n_pairs = c_heads // 2

            def fox_bias(cum, qlo, tq_pad):
                b_, _, s_ = cum.shape
                cq = cum[:, :, qlo:qlo + tq_pad].reshape(b_, n_pairs, 2, tq_pad).transpose(0, 1, 3, 2)
                ck = cum.reshape(b_, n_pairs, 2, s_ // KEY_CHUNK, KEY_CHUNK).transpose(0, 1, 3, 2, 4)
                return cq, ck

            logf_p = _chunk(hp32, O_MISC)[..., :c_heads]
            st_p = [_chunk(hp32, O_K, 8), _chunk(hp32, O_V, 8), logf_p]
            cum_p = _cumsum(logf_p.transpose(0, 2, 1))
            cq, ck = fox_bias(cum_p, 0, t)
            o_p = _fox(hp16, dict(k=(hp16, O_K), v=(hp16, O_V)), cq, ck, O_Q, TQ_PROMPT, 0, t, False, False)

            logf_s = _chunk(hs32, O_MISC)[..., :c_heads]
            st_s = [_chunk(hs32, O_K, 8), _chunk(hs32, O_V, 8), logf_s]
            news = [jnp.swapaxes(_pad_rows(a, TQ_SAMPLE), 1, 2) for a in st_s]
            gk, gv = _gather_t(page_table, [feat_major(cache_c_k), feat_major(cache_c_v)], j, news[:2],
                               (False, False), (BF16, BF16), GATHER_PAGES)
            (glogf,) = _gather_t(page_table, [feat_major(cache_c_logf)], j, news[2:], (False,), (F32,), GATHER_PAGES)
            cum_s = _cumsum(glogf)
            cq, ck = fox_bias(cum_s, past, TQ_SAMPLE)
            o_s = _fox(_pad_rows(hs16, TQ_SAMPLE), dict(k=(gk, 0), v=(gv, 0)), cq, ck, O_Q, TQ_SAMPLE, past, n_keys_s,
                       True, True)
            od_p.append(st_p)
            od_s.append(st_s)
            w_outs = [o_w_out[j].astype(BF16)]
            os_p = [o_p.reshape(bp * t, -1)]
            os_s = [o_s[:, :ts].reshape(bs * ts, -1)]
        wu, wd = w_up[li].astype(BF16), w_down[li].astype(BF16)
        xp = _postmix(xp, os_p, w_outs, norm_ffn[li], wu, wd, TM_ROWS)
        xs = _postmix(xs, os_s, w_outs, norm_ffn[li], wu, wd, bs * ts)

    outs = [xp.reshape(bp, t, d), xs.reshape(bs, ts, d)]
    n_even = len(ev_p)
    for n in range(9):
        for states, b_ in ((ev_p, bp), (ev_s, bs)):
            rows = states[0][n].shape[1]
            tail = (IDX_DIM,) if n == 2 else (A_KV, HEAD_DIM)
            outs.append(jnp.stack([s[n] for s in states]).reshape((n_even, b_, rows) + tail))
    for n in range(3):
        for states, b_, t_ in ((od_p, bp, t), (od_s, bs, ts)):
            tail = (c_heads,) if n == 2 else (c_heads, HEAD_DIM)
            outs.append(jnp.stack([s[n] for s in states]).reshape((len(states), b_, t_) + tail))
    return tuple(outs)
```

```python
import functools

import numpy as np
import jax
import jax.numpy as jnp
from jax import lax
from jax.experimental import pallas as pl
from jax.experimental.pallas import tpu as pltpu

F32 = jnp.float32
BF16 = jnp.bfloat16

HEAD_DIM = 64
A_KV = 2
IDX_HEADS = 4
IDX_DIM = 64
DSA_TOPK = 256
B_KV = 2
CMP_LEN = 32
CMP_STRIDE = 16
SEL_LEN = 64
SEL_TOP = 16
WINDOW = 512
ROPE_THETA = 10000.0
RMS_EPS = 1e-6
NEG_BIG = -1e30
FORCE_SCORE = 1e9
PAGE_SIZE = 128

LANES = 128
BF16_SUBLANES = 16
VMEM_LIMIT = 56 * 1024 * 1024

GATHER_PAGES = 4
KEY_CHUNK = GATHER_PAGES * PAGE_SIZE
WIN_CHUNK = 256
ATT_UNROLL = 2
TQ_PROMPT = 512
TQ_PROMPT_NSA = 512
TQ_SAMPLE = BF16_SUBLANES
TM_ROWS = 256
FF_CHUNK = 1024
LOGF_PAGES = 16

PLAIN, ROPE, NORM_ROPE, NORM, LOGSIG = range(5)

INT_MIN = np.int32(-2 ** 31)
KEY_NINF = np.int32(-2139095041)


def _dot_nt(a, b):
    return lax.dot_general(a, b, (((1,), (1,)), ((), ())), preferred_element_type=F32)


def _iota(shape, dim):
    return lax.broadcasted_iota(jnp.int32, shape, dim)


def _round_up(x, m):
    return -(-x // m) * m


def _div_pow2(x, n):
    assert n & (n - 1) == 0
    if isinstance(x, int):
        return x // n
    return lax.shift_right_arithmetic(x, jnp.int32(n.bit_length() - 1))


def _loop(lo, hi, body, init, static, unroll=1):
    if static:
        carry = init
        for c in range(lo, hi):
            carry = body(c, carry)
        return carry
    if unroll == 1:
        return lax.fori_loop(lo, hi, body, init)

    def trip(i, carry):
        for u in range(unroll):
            carry = body(i * unroll + u, carry)
        return carry

    return lax.fori_loop(_div_pow2(lo, unroll), _div_pow2(hi, unroll), trip, init)


def _round_up_pow2(x, n):
    return _div_pow2(x + (n - 1), n) * n


def _min(a, b):
    return min(a, b) if isinstance(a, int) and isinstance(b, int) else jnp.minimum(a, b)


def _max(a, b):
    return max(a, b) if isinstance(a, int) and isinstance(b, int) else jnp.maximum(a, b)


def _chunk_start(c, ch):
    return c * ch if isinstance(c, int) else pl.multiple_of(c * ch, ch)


def _inproj_body(x_ref, g_ref, w_ref, gains_ref, cos_ref, sin_ref, bias_ref, o32_ref, o16_ref, *, kinds):
    x = x_ref[...]
    xn = x * lax.rsqrt(jnp.mean(x * x, axis=-1, keepdims=True) + RMS_EPS) * g_ref[...]
    xb = xn.astype(BF16)
    cos = cos_ref[...]
    sin = sin_ref[...]
    lane = _iota((1, LANES), 1)
    lo = lane < HEAD_DIM
    first_half = (lane & (HEAD_DIM - 1)) < (HEAD_DIM // 2)
    n_chunks = len(kinds)
    group = 4
    for c0 in range(0, n_chunks, group):
        c1 = min(c0 + group, n_chunks)
        hh = jnp.dot(xb, w_ref[:, c0 * LANES:c1 * LANES], preferred_element_type=F32)
        for c in range(c0, c1):
            h = hh[:, (c - c0) * LANES:(c - c0 + 1) * LANES]
            kind, gi = kinds[c]
            if kind in (NORM_ROPE, NORM):
                ss = h * h
                s_lo = jnp.sum(jnp.where(lo, ss, 0.0), axis=-1, keepdims=True)
                s_hi = jnp.sum(jnp.where(lo, 0.0, ss), axis=-1, keepdims=True)
                ms = jnp.where(lo, s_lo, s_hi) * (1.0 / HEAD_DIM)
                h = h * lax.rsqrt(ms + RMS_EPS) * gains_ref[gi:gi + 1, :]
            if kind in (NORM_ROPE, ROPE):
                partner = jnp.where(first_half, pltpu.roll(h, LANES - HEAD_DIM // 2, 1),
                                    pltpu.roll(h, HEAD_DIM // 2, 1))
                h = h * cos + partner * sin
            if kind == LOGSIG:
                z = h + bias_ref[...]
                h = -(jnp.maximum(-z, 0.0) + jnp.log1p(jnp.exp(-jnp.abs(z))))
            o32_ref[:, c * LANES:(c + 1) * LANES] = h
            o16_ref[:, c * LANES:(c + 1) * LANES] = h.astype(BF16)


def _inproj(x, g, w, gains, cos, sin, bias, kinds, tm):
    n, d = x.shape
    nout = w.shape[1]
    n_tab = cos.shape[0] // tm
    const = lambda i: (0, 0)
    return pl.pallas_call(
        functools.partial(_inproj_body, kinds=kinds),
        grid=(n // tm,),
        in_specs=[
            pl.BlockSpec((tm, d), lambda i: (i, 0)),
            pl.BlockSpec((1, d), const),
            pl.BlockSpec((d, nout), const, pipeline_mode=pl.Buffered(1)),
            pl.BlockSpec(gains.shape, const),
            pl.BlockSpec((tm, LANES), lambda i: (i % n_tab, 0)),
            pl.BlockSpec((tm, LANES), lambda i: (i % n_tab, 0)),
            pl.BlockSpec((1, LANES), const),
        ],
        out_specs=[pl.BlockSpec((tm, nout), lambda i: (i, 0)),
                   pl.BlockSpec((tm, nout), lambda i: (i, 0))],
        out_shape=[jax.ShapeDtypeStruct((n, nout), F32), jax.ShapeDtypeStruct((n, nout), BF16)],
        compiler_params=pltpu.CompilerParams(dimension_semantics=("parallel",), vmem_limit_bytes=VMEM_LIMIT),
        name="inproj",
    )(x, g.reshape(1, d), w, gains, cos, sin, bias)


def _postmix_body(*refs, n_o):
    x_ref = refs[0]
    o_refs = refs[1:1 + n_o]
    w_ref, g_ref, wup_ref, wdn_ref, out_ref = refs[1 + n_o:]
    o = o_refs[0][...] if n_o == 1 else jnp.concatenate([r[...] for r in o_refs], axis=-1)
    x1 = x_ref[...] + jnp.dot(o, w_ref[...], preferred_element_type=F32)
    xn = x1 * lax.rsqrt(jnp.mean(x1 * x1, axis=-1, keepdims=True) + RMS_EPS) * g_ref[...]
    xb = xn.astype(BF16)
    acc = x1
    d_ff = wup_ref.shape[1]
    for c in range(d_ff // FF_CHUNK):
        h = jnp.dot(xb, wup_ref[:, c * FF_CHUNK:(c + 1) * FF_CHUNK], preferred_element_type=F32)
        a = jnp.square(jnp.maximum(h, 0.0)).astype(BF16)
        acc = acc + jnp.dot(a, wdn_ref[c * FF_CHUNK:(c + 1) * FF_CHUNK, :], preferred_element_type=F32)
    out_ref[...] = acc


def _postmix(x, os_, ws, g, w_up, w_down, tm):
    n, d = x.shape
    const = lambda i: (0, 0)
    row = lambda i: (i, 0)
    w_out = ws[0] if len(ws) == 1 else jnp.concatenate(ws, axis=0)
    in_specs = [pl.BlockSpec((tm, d), row)]
    in_specs += [pl.BlockSpec((tm, o.shape[1]), row) for o in os_]
    in_specs += [pl.BlockSpec(w_out.shape, const, pipeline_mode=pl.Buffered(1)),
                 pl.BlockSpec((1, d), const),
                 pl.BlockSpec(w_up.shape, const, pipeline_mode=pl.Buffered(1)),
                 pl.BlockSpec(w_down.shape, const, pipeline_mode=pl.Buffered(1))]
    return pl.pallas_call(
        functools.partial(_postmix_body, n_o=len(os_)),
        grid=(n // tm,),
        in_specs=in_specs,
        out_specs=pl.BlockSpec((tm, d), row),
        out_shape=jax.ShapeDtypeStruct((n, d), F32),
        compiler_params=pltpu.CompilerParams(dimension_semantics=("parallel",), vmem_limit_bytes=VMEM_LIMIT),
        name="postmix",
    )(x, *os_, w_out, g.reshape(1, d), w_up, w_down)


def _gather_t_body(pt_ref, *refs, n, pps, n_steps, dup, has_new, key_major):
    del pt_ref
    page_refs = refs[:n * pps]
    new_refs = refs[n * pps:n * pps + (n if has_new else 0)]
    out_refs = refs[n * pps + len(new_refs):]

    def widen(val, a):
        return jnp.concatenate([val, val], axis=0) if dup[a] else val

    def copy_pages():
        for a in range(n):
            for u in range(pps):
                page = widen(page_refs[a * pps + u][...], a)
                where = slice(u * PAGE_SIZE, (u + 1) * PAGE_SIZE)
                if key_major:
                    out_refs[a][where, :] = page.T.astype(out_refs[a].dtype)
                else:
                    out_refs[a][:, where] = page.astype(out_refs[a].dtype)

    if not has_new:
        copy_pages()
        return
    p = pl.program_id(1)
    pl.when(p < n_steps)(copy_pages)

    @pl.when(p == n_steps)
    def _():
        for a in range(n):
            new = new_refs[a][...].astype(out_refs[a].dtype)
            if key_major:
                r = new.shape[0]
                out_refs[a][0:r, :] = new
                out_refs[a][r:, :] = jnp.zeros((pps * PAGE_SIZE - r, new.shape[1]), out_refs[a].dtype)
            else:
                new = widen(new, a)
                r = new.shape[1]
                out_refs[a][:, 0:r] = new
                out_refs[a][:, r:] = jnp.zeros((new.shape[0], pps * PAGE_SIZE - r), out_refs[a].dtype)


def _gather_t(page_table, caches, layer, news, dup, out_dtypes, pps, key_major=False):
    bs, n_pages = page_table.shape
    assert n_pages % pps == 0
    n_steps = n_pages // pps
    n = len(caches)
    has_new = news is not None
    in_specs, out_specs, out_shape = [], [], []
    for c in caches:
        for u in range(pps):
            in_specs.append(pl.BlockSpec(
                (None, None, c.shape[2], PAGE_SIZE),
                lambda b, p, pt, u=u: (layer, pt[b, jnp.minimum(p, n_steps - 1) * pps + u], 0, 0)))
    if has_new:
        for nw in news:
            in_specs.append(pl.BlockSpec((None,) + nw.shape[1:], lambda b, p, pt: (b, 0, 0)))
    s_out = (n_steps + has_new) * pps * PAGE_SIZE
    for c, d_, dt in zip(caches, dup, out_dtypes):
        feat = c.shape[2] * (2 if d_ else 1)
        if key_major:
            out_specs.append(pl.BlockSpec((None, pps * PAGE_SIZE, feat), lambda b, p, pt: (b, p, 0)))
            out_shape.append(jax.ShapeDtypeStruct((bs, s_out, feat), dt))
        else:
            out_specs.append(pl.BlockSpec((None, feat, pps * PAGE_SIZE), lambda b, p, pt: (b, 0, p)))
            out_shape.append(jax.ShapeDtypeStruct((bs, feat, s_out), dt))
    args = [a for c in caches for a in [c] * pps] + (list(news) if has_new else [])
    return pl.pallas_call(
        functools.partial(_gather_t_body, n=n, pps=pps, n_steps=n_steps, dup=tuple(dup), has_new=has_new,
                          key_major=key_major),
        grid_spec=pltpu.PrefetchScalarGridSpec(
            num_scalar_prefetch=1, grid=(bs, n_steps + has_new), in_specs=in_specs, out_specs=out_specs),
        out_shape=out_shape,
        compiler_params=pltpu.CompilerParams(dimension_semantics=("parallel", "arbitrary"),
                                             vmem_limit_bytes=VMEM_LIMIT),
        name="paged_gather_t",
    )(page_table, *args)


def _float_key(s):
    b = lax.bitcast_convert_type(s + 0.0, jnp.int32)
    return jnp.where(b >= 0, b, b ^ jnp.int32(0x7FFFFFFF))


def _prefix_tri():
    return jnp.where(_iota((LANES, LANES), 0) <= _iota((LANES, LANES), 1), 1.0, 0.0).astype(BF16)


def _topk_mask(key_ref, sel_ref, k, tri, n_used, static, additive, work=None):
    _, rows, width = key_ref.shape
    pieces = range(width // LANES)
    on, off = (0.0, NEG_BIG) if additive else (1.0, 0.0)
    kf = float(k)

    if static:
        def count(pred):
            def body(c, acc):
                for u in pieces:
                    acc = acc + jnp.where(pred(key_ref[c, :, u * LANES:(u + 1) * LANES]), 1.0, 0.0)
                return acc
            return jnp.sum(_loop(0, n_used, body, jnp.zeros((rows, LANES), F32), True), axis=-1, keepdims=True)

        zero = jnp.zeros((rows, 1), jnp.int32)
        t = jnp.where(count(lambda kc: kc >= zero) >= kf, zero, jnp.full((rows, 1), INT_MIN, jnp.int32))

        def bit_step(i, t):
            cand = t | jnp.left_shift(jnp.int32(1), 30 - i)
            return jnp.where(count(lambda kc: kc >= cand) >= kf, cand, t)

        t = lax.fori_loop(0, 31, bit_step, t)
        need = kf - count(lambda kc: kc > t)
    else:
        thr_ref, cnt_ref = work

        def count(strict):
            cnt_ref[...] = jnp.zeros((rows, LANES), F32)

            def body(c, _):
                cand = thr_ref[1]
                part = jnp.zeros((rows, LANES), F32)
                for u in pieces:
                    kc = key_ref[c, :, u * LANES:(u + 1) * LANES]
                    part = part + jnp.where((kc > cand) if strict else (kc >= cand), 1.0, 0.0)
                cnt_ref[...] += part
                return 0

            lax.fori_loop(0, n_used, body, 0)
            return jnp.sum(cnt_ref[...], axis=-1, keepdims=True)

        thr_ref[1] = jnp.zeros((rows, LANES), jnp.int32)
        thr_ref[0] = jnp.where(count(False) >= kf, thr_ref[1], jnp.full((rows, LANES), INT_MIN, jnp.int32))
        for i in range(31):
            thr_ref[1] = thr_ref[0] | jnp.int32(1 << (30 - i))
            thr_ref[0] = jnp.where(count(False) >= kf, thr_ref[1], thr_ref[0])
        thr_ref[1] = thr_ref[0]
        need = kf - count(True)
        t = thr_ref[0][:, 0:1]

    def mark(c, carry):
        for u in pieces:
            kc = key_ref[c, :, u * LANES:(u + 1) * LANES]
            eq = kc == t
            pc = jnp.dot(jnp.where(eq, 1.0, 0.0).astype(BF16), tri, preferred_element_type=F32) + carry
            sel = ((kc > t) | (eq & (pc <= need))) & (kc > KEY_NINF)
            sel_ref[c, :, u * LANES:(u + 1) * LANES] = jnp.where(sel, on, off)
            carry = jnp.broadcast_to(pc[:, LANES - 1:LANES], (rows, LANES))
        return carry

    _loop(0, n_used, mark, jnp.zeros((rows, LANES), F32), static)


def _key_chunk(ref, c, ch, kv_t):
    keys = pl.ds(_chunk_start(c, ch), ch)
    return ref[0, :, keys] if kv_t else ref[0, keys, :]


def _qk(q, kc, kv_t):
    return jnp.dot(q, kc, preferred_element_type=F32) if kv_t else _dot_nt(q, kc)


def _attend_pair(q2, k_ref, v_ref, s_ref, bounds, ch, mask_fn, static, unroll=1, kidx=None, kv_t=False):
    c_lo, c_mid, c_hi = bounds
    tq = q2.shape[0]
    pieces = range(ch // LANES)
    lo = _iota((1, LANES), 1) < HEAD_DIM
    zero = jnp.zeros_like(q2)
    qs = (jnp.where(lo, q2, zero), jnp.where(lo, zero, q2))
    chunk_of = lambda ref, c: _key_chunk(ref, c if kidx is None else kidx(c), ch, kv_t)

    def score(c, mvecs, diag):
        kc = chunk_of(k_ref, c)
        out = []
        for half in range(2):
            s = _qk(qs[half], kc, kv_t)
            extra = mask_fn(c, half, diag)
            if extra is not None:
                s = s + extra
            s_ref[half, c] = s
            mvec = mvecs[half]
            for u in pieces:
                mvec = jnp.maximum(mvec, s[:, u * LANES:(u + 1) * LANES])
            out.append(mvec)
        return tuple(out)

    mvecs = (jnp.full((tq, LANES), NEG_BIG, F32),) * 2
    mvecs = _loop(c_lo, c_mid, functools.partial(score, diag=False), mvecs, static, unroll)
    mvecs = _loop(c_mid, c_hi, functools.partial(score, diag=True), mvecs, static, unroll)
    ms = [jnp.max(mv, axis=-1, keepdims=True) for mv in mvecs]

    def accumulate(c, carry):
        vc = chunk_of(v_ref, c)
        out = []
        for half in range(2):
            lvec, acc = carry[2 * half:2 * half + 2]
            p = jnp.exp(s_ref[half, c] - ms[half])
            for u in pieces:
                lvec = lvec + p[:, u * LANES:(u + 1) * LANES]
            pv = _dot_nt(p.astype(BF16), vc) if kv_t else jnp.dot(p.astype(BF16), vc, preferred_element_type=F32)
            out += [lvec, acc + pv]
        return tuple(out)

    res = _loop(c_lo, c_hi, accumulate, (jnp.zeros((tq, LANES), F32),) * 4, static, unroll)
    outs = []
    for half in range(2):
        l = jnp.sum(res[2 * half], axis=-1, keepdims=True)
        outs.append(jnp.where(ms[half] > 0.5 * NEG_BIG, res[2 * half + 1] / jnp.maximum(l, 1e-30), 0.0))
    return jnp.where(lo, outs[0], outs[1])


def _attend_stacked(q2s, k_ref, v_ref, s_ref, n_chunks, ch, mask_fn, kv_t, kidx=None):
    tq = q2s[0].shape[0]
    n = len(q2s)
    rows = 2 * n * tq
    pieces = range(ch // LANES)
    lo = _iota((1, LANES), 1) < HEAD_DIM
    blocks = []
    for q2 in q2s:
        zero = jnp.zeros_like(q2)
        blocks += [jnp.where(lo, q2, zero), jnp.where(lo, zero, q2)]
    qst = jnp.concatenate(blocks, axis=0)
    chunk_of = lambda ref, c: _key_chunk(ref, c if kidx is None else kidx(c), ch, kv_t)

    mvec = jnp.full((rows, LANES), NEG_BIG, F32)
    for c in range(n_chunks):
        s = _qk(qst, chunk_of(k_ref, c), kv_t)
        s = s + jnp.concatenate([mask_fn(c, 0, True), mask_fn(c, 1, True)] * n, axis=0)
        s_ref[c] = s
        for u in pieces:
            mvec = jnp.maximum(mvec, s[:, u * LANES:(u + 1) * LANES])
    m = jnp.max(mvec, axis=-1, keepdims=True)
    lvec = jnp.zeros((rows, LANES), F32)
    acc = jnp.zeros((rows, LANES), F32)
    for c in range(n_chunks):
        p = jnp.exp(s_ref[c] - m)
        for u in pieces:
            lvec = lvec + p[:, u * LANES:(u + 1) * LANES]
        vc = chunk_of(v_ref, c)
        acc = acc + (_dot_nt(p.astype(BF16), vc) if kv_t else jnp.dot(p.astype(BF16), vc, preferred_element_type=F32))
    l = jnp.sum(lvec, axis=-1, keepdims=True)
    o = jnp.where(m > 0.5 * NEG_BIG, acc / jnp.maximum(l, 1e-30), 0.0)
    return [jnp.where(lo, o[2 * j * tq:(2 * j + 1) * tq], o[(2 * j + 1) * tq:(2 * j + 2) * tq]) for j in range(n)]


def _scaled(q, scale):
    return (q.astype(F32) * scale).astype(BF16)


def _query_block(qpos0, tq, axis, static):
    q0 = qpos0 if static else qpos0 + pl.program_id(axis) * tq
    return q0, q0 + _iota((tq, 1), 0)


def _causal_chunks(q0, tq, n_ch, ch):
    c_hi = _min(n_ch, _div_pow2(q0 + tq - 1, ch) + 1)
    c_mid = _min(c_hi, _div_pow2(q0 + 1, ch))
    if isinstance(c_hi, int):
        return c_mid, c_hi
    assert n_ch % ATT_UNROLL == 0
    return _div_pow2(c_mid, ATT_UNROLL) * ATT_UNROLL, _round_up_pow2(c_hi, ATT_UNROLL)


def _dsa_body(qa_ref, iq_ref, misc_ref, k_ref, v_ref, ik_ref, o_ref, key_ref, sel_ref, s_ref, thr_ref, cnt_ref,
              *, tq, ch, topk, qpos0, n_keys, static, kv_t):
    n_ch = key_ref.shape[0]
    q0, qpos = _query_block(qpos0, tq, 1, static)
    _, c_hi = _causal_chunks(q0, tq, n_ch, ch)
    lo = _iota((1, LANES), 1) < HEAD_DIM
    iw = misc_ref[0][:, 0:IDX_HEADS]
    tri = _prefix_tri()

    iq_heads = []
    for hp in range(IDX_HEADS // 2):
        iqc = iq_ref[0, :, hp * LANES:(hp + 1) * LANES]
        zero = jnp.zeros_like(iqc)
        iq_heads += [jnp.where(lo, iqc, zero), jnp.where(lo, zero, iqc)]
    if static:
        iq_heads = [jnp.concatenate(iq_heads, axis=0)]

    def score_body(c, _):
        ikc = _key_chunk(ik_ref, c, ch, kv_t)
        acc = jnp.zeros((tq, ch), F32)
        for i, qm in enumerate(iq_heads):
            s_all = _qk(qm, ikc, kv_t) * (IDX_DIM ** -0.5)
            for r in range(qm.shape[0] // tq):
                h = i + r
                acc = acc + jnp.maximum(s_all[r * tq:(r + 1) * tq], 0.0) * iw[:, h:h + 1]
        sc = acc * (IDX_HEADS ** -0.5)
        kpos = c * ch + _iota((1, ch), 1)
        key_ref[c] = jnp.where((kpos <= qpos) & (kpos < n_keys), _float_key(sc), KEY_NINF)
        return 0

    _loop(0, c_hi, score_body, 0, static, ATT_UNROLL)
    _topk_mask(key_ref, sel_ref, topk, tri, c_hi, static, additive=True, work=(thr_ref, cnt_ref))

    n_pairs = qa_ref.shape[2] // LANES
    q2s = [_scaled(qa_ref[0, :, j * LANES:(j + 1) * LANES], HEAD_DIM ** -0.5) for j in range(n_pairs)]
    sel_mask = lambda c, half, diag: sel_ref[c]
    if static:
        outs = _attend_stacked(q2s, k_ref, v_ref, s_ref, c_hi, ch, sel_mask, kv_t)
    for j in range(n_pairs):
        o = outs[j] if static else _attend_pair(q2s[j], k_ref, v_ref, s_ref, (0, 0, c_hi), ch, sel_mask,
                                                static, ATT_UNROLL, kv_t=kv_t)
        o_ref[0, :, j * LANES:(j + 1) * LANES] = o.astype(o_ref.dtype)


def _stage_shape(n_slots, tq, ch, static):
    return (n_slots, 8 * tq, ch) if static else (2, n_slots, tq, ch)


def _kv_len(arr, kv_t):
    return arr.shape[2] if kv_t else arr.shape[1]


def _kv_spec(operand, kv_t, chunk_of_step=lambda *g: 0):
    arr, c0 = operand
    s = _kv_len(arr, kv_t)
    if kv_t:
        return pl.BlockSpec((1, LANES, s), lambda bi, *g: (bi, c0 + chunk_of_step(*g), 0))
    return pl.BlockSpec((1, s, LANES), lambda bi, *g: (bi, 0, c0 + chunk_of_step(*g)))


def _dsa(q16, m32, src, col, tq, topk, qpos0, n_keys, static, kv_t):
    b, t, _ = q16.shape
    s = _kv_len(src["k"][0], kv_t)
    ch = KEY_CHUNK
    n_qa = 4 * LANES
    kv_spec = lambda name: _kv_spec(src[name], kv_t)
    return pl.pallas_call(
        functools.partial(_dsa_body, tq=tq, ch=ch, topk=topk, qpos0=qpos0, n_keys=n_keys, static=static,
                          kv_t=kv_t),
        grid=(b, t // tq),
        in_specs=[
            pl.BlockSpec((1, tq, n_qa), lambda bi, i: (bi, i, col["qa"] * LANES // n_qa)),
            pl.BlockSpec((1, tq, 2 * LANES), lambda bi, i: (bi, i, col["iq"] // 2)),
            pl.BlockSpec((1, tq, LANES), lambda bi, i: (bi, i, col["misc"])),
            kv_spec("k"), kv_spec("v"), kv_spec("ik"),
        ],
        out_specs=pl.BlockSpec((1, tq, n_qa), lambda bi, i: (bi, i, 0)),
        out_shape=jax.ShapeDtypeStruct((b, t, n_qa), BF16),
        scratch_shapes=[pltpu.VMEM((s // ch, tq, ch), jnp.int32), pltpu.VMEM((s // ch, tq, ch), F32),
                        pltpu.VMEM(_stage_shape(s // ch, tq, ch, static), F32),
                        pltpu.VMEM((2, tq, LANES), jnp.int32), pltpu.VMEM((tq, LANES), F32)],
        compiler_params=pltpu.CompilerParams(dimension_semantics=("parallel", "arbitrary"),
                                             vmem_limit_bytes=VMEM_LIMIT),
        name="dsa",
    )(q16, q16, m32, src["k"][0], src["v"][0], src["ik"][0])


def _nsa_body(qb_ref, misc_ref, cmpk_ref, cmpv_ref, sk_ref, sv_ref, wk_ref, wv_ref, imp_ref, exp_ref,
              o_ref, key_ref, selb_ref, msk_ref, s_ref, sw_ref,
              *, tq, ch, n_sel, qpos0, wpos0, n_wvalid, static, kv_t):
    n_ch = exp_ref.shape[0]
    ncp = cmpk_ref.shape[1]
    nsp = imp_ref.shape[1]
    q0, qpos = _query_block(qpos0, tq, 1, static)
    _, c_hi = _causal_chunks(q0, tq, n_ch, ch)
    lo = _iota((1, LANES), 1) < HEAD_DIM
    scale = HEAD_DIM ** -0.5
    tri = _prefix_tri()
    n_chunks_q = qb_ref.shape[2] // LANES
    gates = jax.nn.sigmoid(misc_ref[0])

    cvalid = (_iota((1, ncp), 1) * CMP_STRIDE + (CMP_LEN - 1)) <= qpos
    ck = cmpk_ref[0]
    cv = cmpv_ref[0]
    imp_c = [jnp.zeros((tq, ncp), F32), jnp.zeros((tq, ncp), F32)]
    o_cmp = []
    for j in range(n_chunks_q):
        q2 = qb_ref[0, :, j * LANES:(j + 1) * LANES]
        zero = jnp.zeros_like(q2)
        outs = []
        for half in range(2):
            qm = jnp.where(lo, q2, zero) if half == 0 else jnp.where(lo, zero, q2)
            s = jnp.where(cvalid, _dot_nt(qm, ck) * scale, NEG_BIG)
            p = jnp.where(cvalid, jnp.exp(s - jnp.max(s, axis=-1, keepdims=True)), 0.0)
            pn = p / jnp.maximum(jnp.sum(p, axis=-1, keepdims=True), 1e-30)
            imp_c[half] = imp_c[half] + pn
            outs.append(jnp.dot(pn.astype(BF16), cv, preferred_element_type=F32))
        o_cmp.append(jnp.where(lo, outs[0], outs[1]))

    blk = _iota((1, nsp), 1)
    adm = blk * SEL_LEN <= qpos
    cur = _div_pow2(qpos, SEL_LEN)
    forced = adm & ((blk == 0) | (blk == cur) | (blk == cur - 1))
    for g in range(2):
        imp = jnp.dot(imp_c[g], imp_ref[...], preferred_element_type=F32, precision=lax.Precision.HIGHEST)
        score = jnp.where(forced, FORCE_SCORE, jnp.where(adm, imp, -jnp.inf))
        key_ref[0, g * tq:(g + 1) * tq, :] = _float_key(score)
    _topk_mask(key_ref, selb_ref, n_sel, tri, 1, True, additive=False)
    for g in range(2):
        bm = selb_ref[0, g * tq:(g + 1) * tq, :].astype(BF16)

        def expand(c, _):
            kpos = c * ch + _iota((1, ch), 1)
            hit = (jnp.dot(bm, exp_ref[c], preferred_element_type=F32) > 0.5) & (kpos <= qpos)
            msk_ref[g, c] = jnp.where(hit, 0.0, NEG_BIG).astype(msk_ref.dtype)
            return 0

        _loop(0, c_hi, expand, 0, static, ATT_UNROLL)

    wch = WIN_CHUNK
    n_wch = wk_ref.shape[2 if kv_t else 1] // wch
    n_win = sw_ref.shape[0 if static else 1]
    wc0 = _div_pow2(q0 - WINDOW - wpos0, wch)

    def win_chunk(u):
        c = wc0 + u
        return _min(_max(c, 0), n_wch - 1)

    def win_mask(u, half, diag):
        row = (wc0 + u) * wch + _iota((1, wch), 1)
        kpos = wpos0 + row
        dlt = qpos - kpos
        ok = (dlt >= 0) & (dlt <= WINDOW) & (kpos >= 0) & (row >= 0) & (row < n_wvalid)
        return jnp.where(ok, 0.0, NEG_BIG)

    q2s = [_scaled(qb_ref[0, :, j * LANES:(j + 1) * LANES], scale) for j in range(n_chunks_q)]
    sel_mask = lambda c, half, diag: msk_ref[half, c].astype(F32)
    if static:
        o_sels = _attend_stacked(q2s, sk_ref, sv_ref, s_ref, c_hi, ch, sel_mask, kv_t)
        o_wins = _attend_stacked(q2s, wk_ref, wv_ref, sw_ref, n_win, wch, win_mask, kv_t, kidx=win_chunk)
    for j in range(n_chunks_q):
        if static:
            o_sel, o_win = o_sels[j], o_wins[j]
        else:
            o_sel = _attend_pair(q2s[j], sk_ref, sv_ref, s_ref, (0, 0, c_hi), ch, sel_mask, static, ATT_UNROLL,
                                 kv_t=kv_t)
            o_win = _attend_pair(q2s[j], wk_ref, wv_ref, sw_ref, (0, 0, n_win), wch, win_mask, True, kidx=win_chunk,
                                 kv_t=kv_t)
        o = jnp.zeros((tq, LANES), F32)
        for bi, ob in enumerate((o_cmp[j], o_sel, o_win)):
            c_a = IDX_HEADS + 3 * j + bi
            c_b = IDX_HEADS + 3 * (n_chunks_q + j) + bi
            o = o + jnp.where(lo, gates[:, c_a:c_a + 1], gates[:, c_b:c_b + 1]) * ob
        o_ref[0, :, j * LANES:(j + 1) * LANES] = o.astype(o_ref.dtype)


def _nsa(q16, m32, cmpk, cmpv, src, imp_mat, exp_mat, col, tq, n_sel, qpos0, wpos0, n_wvalid, static, kv_t):
    b, t, _ = q16.shape
    s = _kv_len(src["sk"][0], kv_t)
    ch = exp_mat.shape[2]
    ncp, nsp = imp_mat.shape
    n_qb = 4 * LANES
    assert t == tq or tq % WIN_CHUNK == 0
    mis = (qpos0 - WINDOW - wpos0) % WIN_CHUNK
    n_win = (mis + WINDOW + tq - 1) // WIN_CHUNK + 1
    spec = lambda name: _kv_spec(src[name], kv_t)
    return pl.pallas_call(
        functools.partial(_nsa_body, tq=tq, ch=ch, n_sel=n_sel, qpos0=qpos0, wpos0=wpos0, n_wvalid=n_wvalid,
                          static=static, kv_t=kv_t),
        grid=(b, t // tq),
        in_specs=[
            pl.BlockSpec((1, tq, n_qb), lambda bi, i: (bi, i, col["qb"] * LANES // n_qb)),
            pl.BlockSpec((1, tq, LANES), lambda bi, i: (bi, i, col["misc"])),
            pl.BlockSpec((1, ncp, LANES), lambda bi, i: (bi, 0, 0)),
            pl.BlockSpec((1, ncp, LANES), lambda bi, i: (bi, 0, 0)),
            spec("sk"), spec("sv"), spec("wk"), spec("wv"),
            pl.BlockSpec((ncp, nsp), lambda bi, i: (0, 0)),
            pl.BlockSpec(exp_mat.shape, lambda bi, i: (0, 0, 0)),
        ],
        out_specs=pl.BlockSpec((1, tq, n_qb), lambda bi, i: (bi, i, 0)),
        out_shape=jax.ShapeDtypeStruct((b, t, n_qb), BF16),
        scratch_shapes=[pltpu.VMEM((1, 2 * tq, nsp), jnp.int32), pltpu.VMEM((1, 2 * tq, nsp), F32),
                        pltpu.VMEM((2, s // ch, tq, ch), BF16), pltpu.VMEM(_stage_shape(s // ch, tq, ch, static), F32),
                        pltpu.VMEM(_stage_shape(n_win, tq, WIN_CHUNK, static), F32)],
        compiler_params=pltpu.CompilerParams(dimension_semantics=("parallel", "arbitrary"),
                                             vmem_limit_bytes=VMEM_LIMIT),
        name="nsa",
    )(q16, m32, cmpk, cmpv, src["sk"][0], src["sv"][0], src["wk"][0], src["wv"][0], imp_mat, exp_mat)


def _compress_body(uk_ref, uv_ref, wk_ref, wv_ref, pe_ref, ok_ref, ov_ref, shift_ref):
    nu = uk_ref.shape[1]
    half = wk_ref.shape[0] // 2
    shift_ref[nu:nu + 8, :] = jnp.zeros((8, LANES), F32)
    for u_ref, w_ref, o_ref in ((uk_ref, wk_ref, ok_ref), (uv_ref, wv_ref, ov_ref)):
        u = u_ref[0]
        top = jnp.dot(u, w_ref[0:half, :], preferred_element_type=F32)
        shift_ref[0:nu, :] = jnp.dot(u, w_ref[half:2 * half, :], preferred_element_type=F32)
        pec = jnp.dot(pe_ref[...], w_ref[...], preferred_element_type=F32)[0:1, :]
        o_ref[0] = (top + shift_ref[1:nu + 1, :] + pec).astype(o_ref.dtype)


def _compress(uk, uv, wk2, wv2, pe2):
    b, nu, width = uk.shape
    const = lambda bi: (0, 0)
    row = lambda bi: (bi, 0, 0)
    return pl.pallas_call(
        _compress_body,
        grid=(b,),
        in_specs=[pl.BlockSpec((1, nu, width), row), pl.BlockSpec((1, nu, width), row),
                  pl.BlockSpec(wk2.shape, const), pl.BlockSpec(wv2.shape, const), pl.BlockSpec(pe2.shape, const)],
        out_specs=[pl.BlockSpec((1, nu, LANES), row), pl.BlockSpec((1, nu, LANES), row)],
        out_shape=[jax.ShapeDtypeStruct((b, nu, LANES), BF16)] * 2,
        scratch_shapes=[pltpu.VMEM((nu + 8, LANES), F32)],
        compiler_params=pltpu.CompilerParams(dimension_semantics=("parallel",), vmem_limit_bytes=VMEM_LIMIT),
        name="compress",
    )(uk, uv, wk2, wv2, pe2)


def _cumsum_body(x_ref, o_ref):
    s = x_ref.shape[2]
    tri = jnp.where(_iota((LANES, LANES), 0) <= _iota((LANES, LANES), 1), 1.0, 0.0)
    carry = jnp.zeros((x_ref.shape[1], 1), F32)
    for c in range(s // LANES):
        blk = x_ref[0, :, c * LANES:(c + 1) * LANES]
        pc = jnp.dot(blk, tri, preferred_element_type=F32, precision=lax.Precision.HIGHEST) + carry
        o_ref[0, :, c * LANES:(c + 1) * LANES] = pc
        carry = pc[:, LANES - 1:LANES]


def _cumsum(x):
    b, h, s = x.shape
    return pl.pallas_call(
        _cumsum_body,
        grid=(b,),
        in_specs=[pl.BlockSpec((1, h, s), lambda bi: (bi, 0, 0))],
        out_specs=pl.BlockSpec((1, h, s), lambda bi: (bi, 0, 0)),
        out_shape=jax.ShapeDtypeStruct((b, h, s), F32),
        compiler_params=pltpu.CompilerParams(dimension_semantics=("parallel",)),
        name="cumsum",
    )(x)


def _fox_body(q_ref, k_ref, v_ref, cq_ref, ck_ref, o_ref, s_ref, *, tq, ch, qpos0, n_keys, static, kv_t):
    n_ch = ck_ref.shape[2]
    q0, qpos = _query_block(qpos0, tq, 2, static)
    c_mid, c_hi = _causal_chunks(q0, tq, n_ch, ch)
    cq = cq_ref[0, 0]

    def bias_fn(c, half, diag):
        bias = cq[:, half:half + 1] - ck_ref[0, 0, c, half:half + 1, :]
        if diag:
            kpos = c * ch + _iota((1, ch), 1)
            bias = jnp.where((kpos <= qpos) & (kpos < n_keys), bias, NEG_BIG)
        return bias

    q2 = _scaled(q_ref[0], HEAD_DIM ** -0.5)
    o = _attend_pair(q2, k_ref, v_ref, s_ref, (0, c_mid, c_hi), ch, bias_fn, static, ATT_UNROLL, kv_t=kv_t)
    o_ref[0] = o.astype(o_ref.dtype)


def _fox(q16, src, cq, ck, qcol, tq, qpos0, n_keys, static, kv_t):
    b, t, _ = q16.shape
    s = _kv_len(src["k"][0], kv_t)
    n_pairs = cq.shape[1]
    ch = ck.shape[4]
    kv_spec = lambda name: _kv_spec(src[name], kv_t, lambda hp, i: hp)
    return pl.pallas_call(
        functools.partial(_fox_body, tq=tq, ch=ch, qpos0=qpos0, n_keys=n_keys, static=static, kv_t=kv_t),
        grid=(b, n_pairs, t // tq),
        in_specs=[
            pl.BlockSpec((1, tq, LANES), lambda bi, hp, i: (bi, i, qcol + hp)),
            kv_spec("k"), kv_spec("v"),
            pl.BlockSpec((1, 1, tq, 2), lambda bi, hp, i: (bi, hp, i, 0)),
            pl.BlockSpec((1, 1, s // ch, 2, ch), lambda bi, hp, i: (bi, hp, 0, 0, 0)),
        ],
        out_specs=pl.BlockSpec((1, tq, LANES), lambda bi, hp, i: (bi, i, hp)),
        out_shape=jax.ShapeDtypeStruct((b, t, n_pairs * LANES), BF16),
        scratch_shapes=[pltpu.VMEM((2, s // ch, tq, ch), F32)],
        compiler_params=pltpu.CompilerParams(dimension_semantics=("parallel", "parallel", "arbitrary"),
                                             vmem_limit_bytes=VMEM_LIMIT),
        name="fox",
    )(q16, src["k"][0], src["v"][0], cq, ck)


E_QA, E_KA, E_VA, E_IQ, E_QB, E_IK, E_CK, E_CV, E_SK, E_SV, E_WK, E_WV, E_MISC = 0, 4, 5, 6, 8, 12, 13, 14, 15, 16, 17, 18, 19
G_QA, G_KA, G_QB, G_CMP, G_SEL, G_WIN = range(6)
E_KINDS = ((NORM_ROPE, G_QA),) * 4 + ((NORM_ROPE, G_KA), (PLAIN, 0)) + ((ROPE, 0),) * 2 + ((NORM_ROPE, G_QB),) * 4 + (
    (ROPE, 0), (NORM_ROPE, G_CMP), (PLAIN, 0), (NORM_ROPE, G_SEL), (PLAIN, 0), (NORM_ROPE, G_WIN), (PLAIN, 0), (PLAIN, 0))
O_Q, O_K, O_V, O_MISC = 0, 8, 16, 24
O_KINDS = ((NORM, 0),) * 8 + ((NORM, 1),) * 8 + ((PLAIN, 0),) * 8 + ((LOGSIG, 0),)


def _pair_heads(w, n_groups):
    d = w.shape[0]
    nj = w.shape[1] // (n_groups * HEAD_DIM)
    return w.reshape(d, n_groups, nj, HEAD_DIM).transpose(0, 2, 1, 3).reshape(d, -1)


def _even_weight(w_in):
    d = w_in.shape[0]
    sizes = (512, 128, 128, 256, 64, 4, 512, 128, 128, 128, 128, 128, 128, 24)
    offs = np.cumsum((0,) + sizes)
    qa, ka, va, iq, ik, iw, qb, ck, cv, sk, sv, wk, wv, gl = (w_in[:, offs[i]:offs[i + 1]] for i in range(14))
    misc = jnp.concatenate([iw, gl, jnp.zeros((d, LANES - 28), w_in.dtype)], axis=1)
    cols = [_pair_heads(qa, A_KV), ka, va, iq, _pair_heads(qb, B_KV), ik, ik, ck, cv, sk, sv, wk, wv, misc]
    return jnp.concatenate(cols, axis=1).astype(BF16)


def _odd_weight(w_in):
    d = w_in.shape[0]
    n = O_MISC * LANES
    fl = w_in[:, n:]
    return jnp.concatenate([w_in[:, :n], fl, jnp.zeros((d, LANES - fl.shape[1]), w_in.dtype)], axis=1).astype(BF16)


def _gain_table(gains):
    rows = [jnp.tile(g, 2) for g in gains]
    rows += [jnp.zeros((LANES,), F32)] * (8 - len(rows))
    return jnp.stack(rows)


def _rope_tables(pos):
    half = HEAD_DIM // 2
    inv = ROPE_THETA ** (-jnp.arange(half, dtype=F32) / half)
    ang = pos.astype(F32)[:, None] * inv[None, :]
    cos, sin = jnp.cos(ang), jnp.sin(ang)
    return jnp.tile(jnp.concatenate([cos, cos], axis=1), (1, 2)), jnp.tile(jnp.concatenate([-sin, sin], axis=1), (1, 2))


def _pair_rows(w_out_half, n_groups):
    nj = w_out_half.shape[0] // (n_groups * HEAD_DIM)
    return w_out_half.reshape(n_groups, nj, HEAD_DIM, -1).transpose(1, 0, 2, 3).reshape(w_out_half.shape)


def _compress_weight(w):
    w3 = w.reshape(CMP_LEN, HEAD_DIM, HEAD_DIM)
    eye = jnp.eye(B_KV, dtype=w.dtype)
    w5 = w3[:, None, :, None, :] * eye[None, :, None, :, None]
    return w5.reshape(CMP_LEN * B_KV * HEAD_DIM, B_KV * HEAD_DIM).astype(BF16)


def _importance_matrix(nc, ncp, ns, nsp):
    r, m = SEL_LEN // CMP_STRIDE, CMP_LEN // CMP_STRIDE
    mat = np.zeros((ncp, nsp), np.float32)
    for n in range(ns):
        for a in range(r):
            for b_ in range(m):
                c = n * r - m + 1 + a + b_
                if 0 <= c < nc:
                    mat[c, n] += 1.0
    return jnp.asarray(mat)


def _expand_matrix(nsp, s, ch):
    blk = np.arange(s) // SEL_LEN
    mat = (np.arange(nsp)[:, None] == blk[None, :]).astype(np.float32)
    return jnp.asarray(mat.reshape(nsp, s // ch, ch).transpose(1, 0, 2), dtype=BF16)


def _chunk(a, c, n=1):
    return a[..., c * LANES:(c + n) * LANES]


def _pad_rows(a, rows):
    return jnp.pad(a, ((0, 0), (0, rows - a.shape[1]), (0, 0)))


def kernel(x_prompt, x_sample, cache_a_k, cache_a_v, cache_a_idx, cache_b_cmp_k, cache_b_cmp_v, cache_b_sel_k, cache_b_sel_v, state_b_win_k, state_b_win_v, cache_c_k, cache_c_v, cache_c_logf, page_table, norm_mix, norm_ffn, e_w_in, e_qn_a, e_kn_a, e_qn_b, e_kn_cmp, e_kn_sel, e_kn_win, e_cmp_pos, e_w_cmp_k, e_w_cmp_v, e_w_out, o_w_in, o_b_f, o_qn, o_kn, o_w_out, w_up, w_down):
    bp, t, d = x_prompt.shape
    bs, ts, _ = x_sample.shape
    depth = norm_mix.shape[0]
    n_pages = page_table.shape[1]
    past = n_pages * PAGE_SIZE
    n_keys_s = past + ts
    s_s = past + KEY_CHUNK
    assert ts <= TQ_SAMPLE and t % KEY_CHUNK == 0 and past % KEY_CHUNK == 0
    n_buf = state_b_win_k.shape[2]
    c_heads = cache_c_k.shape[3]

    xp = x_prompt.reshape(bp * t, d)
    xs = x_sample.reshape(bs * ts, d)
    cos_p, sin_p = _rope_tables(jnp.arange(t))
    cos_s, sin_s = (jnp.tile(a, (bs, 1)) for a in _rope_tables(past + jnp.arange(ts)))
    zero_bias = jnp.zeros((1, LANES), F32)

    def feat_major(c):
        if c.ndim == 5:
            return c.transpose(0, 1, 3, 4, 2).reshape(c.shape[:2] + (c.shape[3] * c.shape[4], c.shape[2]))
        return c.transpose(0, 1, 3, 2)

    def key_major(a, heads):
        return a.reshape(a.shape[0], heads, a.shape[1] // heads, a.shape[2]).transpose(0, 3, 1, 2)

    def nsa_consts(n_keys, s):
        nc = (n_keys - CMP_LEN) // CMP_STRIDE + 1
        ncp = _round_up(nc, LANES)
        ns = -(-n_keys // SEL_LEN)
        nsp = _round_up(ns, LANES)
        return ncp, min(SEL_TOP, ns), _importance_matrix(nc, ncp, ns, nsp), _expand_matrix(nsp, s, KEY_CHUNK)

    def fit_rows(a, rows):
        return a[:, :rows] if a.shape[1] >= rows else _pad_rows(a, rows)

    ev_p, ev_s, od_p, od_s = [], [], [], []
    for li in range(depth):
        j = li // 2
        if li % 2 == 0:
            w = _even_weight(e_w_in[j])
            gains = _gain_table([e_qn_a[j], e_kn_a[j], e_qn_b[j], e_kn_cmp[j], e_kn_sel[j], e_kn_win[j]])
            hp32, hp16 = _inproj(xp, norm_mix[li], w, gains, cos_p, sin_p, zero_bias, E_KINDS, TM_ROWS)
            hs32, hs16 = _inproj(xs, norm_mix[li], w, gains, cos_s, sin_s, zero_bias, E_KINDS, bs * ts)
            hp32, hp16 = hp32.reshape(bp, t, -1), hp16.reshape(bp, t, -1)
            hs32, hs16 = hs32.reshape(bs, ts, -1), hs16.reshape(bs, ts, -1)
            wk2, wv2 = _compress_weight(e_w_cmp_k[j]), _compress_weight(e_w_cmp_v[j])
            pe2 = jnp.pad(jnp.tile(e_cmp_pos[j][:, None, :], (1, B_KV, 1)).reshape(1, -1),
                          ((0, BF16_SUBLANES - 1), (0, 0))).astype(BF16)
            cu = CMP_STRIDE * LANES
            col = dict(qa=E_QA, iq=E_IQ, misc=E_MISC, qb=E_QB)

            st_p = [_chunk(hp32, c) for c in (E_KA, E_VA)]
            st_p.append(_chunk(hp32, E_IK)[..., :IDX_DIM])
            st_p += [_chunk(hp32, c) for c in (E_CK, E_CV, E_SK, E_SV)]
            n_keep = min(WINDOW, t)
            st_p += [_chunk(hp32, c)[:, t - n_keep:] for c in (E_WK, E_WV)]
            ncp, n_sel, imp_mat, exp_mat = nsa_consts(t, t)
            cmpk, cmpv = _compress(_chunk(hp16, E_CK).reshape(bp, t // CMP_STRIDE, cu),
                                   _chunk(hp16, E_CV).reshape(bp, t // CMP_STRIDE, cu), wk2, wv2, pe2)
            src = {name: (hp16, c) for name, c in (("k", E_KA), ("v", E_VA), ("ik", E_IK), ("sk", E_SK),
                                                    ("sv", E_SV), ("wk", E_WK), ("wv", E_WV))}
            oa_p = _dsa(hp16, hp32, src, col, TQ_PROMPT, min(DSA_TOPK, t // 4), 0, t, False, False)
            ob_p = _nsa(hp16, hp32, fit_rows(cmpk, ncp), fit_rows(cmpv, ncp), src, imp_mat, exp_mat,
                        col, TQ_PROMPT_NSA, n_sel, 0, 0, t, False, False)

            st_s = [_chunk(hs32, c) for c in (E_KA, E_VA)]
            st_s.append(_chunk(hs32, E_IK)[..., :IDX_DIM])
            st_s += [_chunk(hs32, c) for c in (E_CK, E_CV, E_SK, E_SV)]
            wk_all = jnp.concatenate([feat_major(state_b_win_k)[j], jnp.swapaxes(_chunk(hs32, E_WK), 1, 2)], axis=2)
            wv_all = jnp.concatenate([feat_major(state_b_win_v)[j], jnp.swapaxes(_chunk(hs32, E_WV), 1, 2)], axis=2)
            n_keep = min(WINDOW, n_buf + ts)
            st_s += [key_major(a[:, :, n_buf + ts - n_keep:], B_KV) for a in (wk_all, wv_all)]
            caches = [feat_major(c) for c in (cache_a_k, cache_a_v, cache_a_idx, cache_b_sel_k, cache_b_sel_v)]
            news = [jnp.swapaxes(_pad_rows(st_s[n], TQ_SAMPLE), 1, 2) for n in (0, 1, 2, 5, 6)]
            gk, gv, gik, gsk, gsv = _gather_t(
                page_table, caches, j, news, (False, False, True, False, False), (BF16,) * 5, GATHER_PAGES)
            gck, gcv = _gather_t(page_table, [feat_major(cache_b_cmp_k), feat_major(cache_b_cmp_v)], j,
                                 [_pad_rows(st_s[3], TQ_SAMPLE), _pad_rows(st_s[4], TQ_SAMPLE)],
                                 (False, False), (BF16, BF16), GATHER_PAGES, key_major=True)
            qs16, qs32 = _pad_rows(hs16, TQ_SAMPLE), _pad_rows(hs32, TQ_SAMPLE)
            ncp, n_sel, imp_mat, exp_mat = nsa_consts(n_keys_s, s_s)
            cmpk, cmpv = _compress(gck.reshape(bs, s_s // CMP_STRIDE, cu), gcv.reshape(bs, s_s // CMP_STRIDE, cu),
                                   wk2, wv2, pe2)
            sw = _round_up(n_buf + ts, WIN_CHUNK)
            pad_keys = lambda a: jnp.pad(a, ((0, 0), (0, 0), (0, sw - a.shape[2]))).astype(BF16)
            src = dict(k=(gk, 0), v=(gv, 0), ik=(gik, 0), sk=(gsk, 0), sv=(gsv, 0),
                       wk=(pad_keys(wk_all), 0), wv=(pad_keys(wv_all), 0))
            oa_s = _dsa(qs16, qs32, src, col, TQ_SAMPLE, min(DSA_TOPK, n_keys_s // 4), past, n_keys_s, True, True)
            ob_s = _nsa(qs16, qs32, fit_rows(cmpk, ncp), fit_rows(cmpv, ncp), src, imp_mat, exp_mat,
                        col, TQ_SAMPLE, n_sel, past, past - n_buf, n_buf + ts, True, True)
            ev_p.append(st_p)
            ev_s.append(st_s)
            half = e_w_out.shape[1] // 2
            w_outs = [_pair_rows(e_w_out[j][:half], A_KV).astype(BF16), _pair_rows(e_w_out[j][half:], B_KV).astype(BF16)]
            os_p = [oa_p.reshape(bp * t, -1), ob_p.reshape(bp * t, -1)]
            os_s = [oa_s[:, :ts].reshape(bs * ts, -1), ob_s[:, :ts].reshape(bs * ts, -1)]
        else:
            w = _odd_weight(o_w_in[j])
            gains = _gain_table([o_qn[j], o_kn[j]])
            bias = jnp.pad(o_b_f[j], (0, LANES - c_heads)).reshape(1, LANES)
            hp32, hp16 = _inproj(xp, norm_mix[li], w, gains, cos_p, sin_p, bias, O_KINDS, TM_ROWS)
            hs32, hs16 = _inproj(xs, norm_mix[li], w, gains, cos_s, sin_s, bias, O_KINDS, bs * ts)
            hp32, hp16 = hp32.reshape(bp, t, -1), hp16.reshape(bp, t, -1)
            hs32, hs16 = hs32.reshape(bs, ts, -1), hs16.reshape(bs, ts, -1)
            n_pairs = c_heads // 2

            def fox_bias(cum, qlo, tq_pad):
                b_, _, s_ = cum.shape
                cq = cum[:, :, qlo:qlo + tq_pad].reshape(b_, n_pairs, 2, tq_pad).transpose(0, 1, 3, 2)
                ck = cum.reshape(b_, n_pairs, 2, s_ // KEY_CHUNK, KEY_CHUNK).transpose(0, 1, 3, 2, 4)
                return cq, ck

            logf_p = _chunk(hp32, O_MISC)[..., :c_heads]
            st_p = [_chunk(hp32, O_K, 8), _chunk(hp32, O_V, 8), logf_p]
            cum_p = _cumsum(logf_p.transpose(0, 2, 1))
            cq, ck = fox_bias(cum_p, 0, t)
            o_p = _fox(hp16, dict(k=(hp16, O_K), v=(hp16, O_V)), cq, ck, O_Q, TQ_PROMPT, 0, t, False, False)

            logf_s = _chunk(hs32, O_MISC)[..., :c_heads]
            st_s = [_chunk(hs32, O_K, 8), _chunk(hs32, O_V, 8), logf_s]
            news = [jnp.swapaxes(_pad_rows(a, TQ_SAMPLE), 1, 2) for a in st_s]
            gk, gv = _gather_t(page_table, [feat_major(cache_c_k), feat_major(cache_c_v)], j, news[:2],
                               (False, False), (BF16, BF16), GATHER_PAGES)
            (glogf,) = _gather_t(page_table, [feat_major(cache_c_logf)], j, None, (False,), (F32,), min(LOGF_PAGES, n_pages))
            cum_s = _cumsum(jnp.pad(jnp.concatenate([glogf, news[2]], axis=2),
                                    ((0, 0), (0, 0), (0, KEY_CHUNK - TQ_SAMPLE))))
            cq, ck = fox_bias(cum_s, past, TQ_SAMPLE)
            o_s = _fox(_pad_rows(hs16, TQ_SAMPLE), dict(k=(gk, 0), v=(gv, 0)), cq, ck, O_Q, TQ_SAMPLE, past, n_keys_s,
                       True, True)
            od_p.append(st_p)
            od_s.append(st_s)
            w_outs = [o_w_out[j].astype(BF16)]
            os_p = [o_p.reshape(bp * t, -1)]
            os_s = [o_s[:, :ts].reshape(bs * ts, -1)]
        wu, wd = w_up[li].astype(BF16), w_down[li].astype(BF16)
        xp = _postmix(xp, os_p, w_outs, norm_ffn[li], wu, wd, TM_ROWS)
        xs = _postmix(xs, os_s, w_outs, norm_ffn[li], wu, wd, bs * ts)

    outs = [xp.reshape(bp, t, d), xs.reshape(bs, ts, d)]
    n_even = len(ev_p)
    for n in range(9):
        for states, b_ in ((ev_p, bp), (ev_s, bs)):
            rows = states[0][n].shape[1]
            tail = (IDX_DIM,) if n == 2 else (A_KV, HEAD_DIM)
            outs.append(jnp.stack([s[n] for s in states]).reshape((n_even, b_, rows) + tail))
    for n in range(3):
        for states, b_, t_ in ((od_p, bp, t), (od_s, bs, ts)):
            tail = (c_heads,) if n == 2 else (c_heads, HEAD_DIM)
            outs.append(jnp.stack([s[n] for s in states]).reshape((len(states), b_, t_) + tail))
    return tuple(outs)
```

```python
import functools

import numpy as np
import jax
import jax.numpy as jnp
from jax import lax
from jax.experimental import pallas as pl
from jax.experimental.pallas import tpu as pltpu

F32 = jnp.float32
BF16 = jnp.bfloat16

HEAD_DIM = 64
A_KV = 2
IDX_HEADS = 4
IDX_DIM = 64
DSA_TOPK = 256
B_KV = 2
CMP_LEN = 32
CMP_STRIDE = 16
SEL_LEN = 64
SEL_TOP = 16
WINDOW = 512
ROPE_THETA = 10000.0
RMS_EPS = 1e-6
NEG_BIG = -1e30
FORCE_SCORE = 1e9
PAGE_SIZE = 128

LANES = 128
BF16_SUBLANES = 16
VMEM_LIMIT = 56 * 1024 * 1024

GATHER_PAGES = 4
KEY_CHUNK = GATHER_PAGES * PAGE_SIZE
WIN_CHUNK = 256
ATT_UNROLL = 2
TQ_PROMPT = 512
TQ_PROMPT_NSA = 512
TQ_SAMPLE = BF16_SUBLANES
TM_ROWS = 256
FF_CHUNK = 1024
LOGF_PAGES = 16

PLAIN, ROPE, NORM_ROPE, NORM, LOGSIG = range(5)

INT_MIN = np.int32(-2 ** 31)
KEY_NINF = np.int32(-2139095041)


def _dot_nt(a, b):
    return lax.dot_general(a, b, (((1,), (1,)), ((), ())), preferred_element_type=F32)


def _iota(shape, dim):
    return lax.broadcasted_iota(jnp.int32, shape, dim)


def _round_up(x, m):
    return -(-x // m) * m


def _div_pow2(x, n):
    assert n & (n - 1) == 0
    if isinstance(x, int):
        return x // n
    return lax.shift_right_arithmetic(x, jnp.int32(n.bit_length() - 1))


def _loop(lo, hi, body, init, static, unroll=1):
    if static:
        carry = init
        for c in range(lo, hi):
            carry = body(c, carry)
        return carry
    if unroll == 1:
        return lax.fori_loop(lo, hi, body, init)

    def trip(i, carry):
        for u in range(unroll):
            carry = body(i * unroll + u, carry)
        return carry

    full = _div_pow2(hi, unroll)
    carry = lax.fori_loop(_div_pow2(lo, unroll), full, trip, init)
    return lax.fori_loop(_max(full * unroll, lo), hi, body, carry)


def _min(a, b):
    return min(a, b) if isinstance(a, int) and isinstance(b, int) else jnp.minimum(a, b)


def _max(a, b):
    return max(a, b) if isinstance(a, int) and isinstance(b, int) else jnp.maximum(a, b)


def _chunk_start(c, ch):
    return c * ch if isinstance(c, int) else pl.multiple_of(c * ch, ch)


def _inproj_body(x_ref, g_ref, w_ref, gains_ref, cos_ref, sin_ref, bias_ref, o32_ref, o16_ref, *, kinds):
    x = x_ref[...]
    xn = x * lax.rsqrt(jnp.mean(x * x, axis=-1, keepdims=True) + RMS_EPS) * g_ref[...]
    xb = xn.astype(BF16)
    cos = cos_ref[...]
    sin = sin_ref[...]
    lane = _iota((1, LANES), 1)
    lo = lane < HEAD_DIM
    first_half = (lane & (HEAD_DIM - 1)) < (HEAD_DIM // 2)
    n_chunks = len(kinds)
    group = 4
    for c0 in range(0, n_chunks, group):
        c1 = min(c0 + group, n_chunks)
        hh = jnp.dot(xb, w_ref[:, c0 * LANES:c1 * LANES], preferred_element_type=F32)
        for c in range(c0, c1):
            h = hh[:, (c - c0) * LANES:(c - c0 + 1) * LANES]
            kind, gi = kinds[c]
            if kind in (NORM_ROPE, NORM):
                ss = h * h
                s_lo = jnp.sum(jnp.where(lo, ss, 0.0), axis=-1, keepdims=True)
                s_hi = jnp.sum(jnp.where(lo, 0.0, ss), axis=-1, keepdims=True)
                ms = jnp.where(lo, s_lo, s_hi) * (1.0 / HEAD_DIM)
                h = h * lax.rsqrt(ms + RMS_EPS) * gains_ref[gi:gi + 1, :]
            if kind in (NORM_ROPE, ROPE):
                partner = jnp.where(first_half, pltpu.roll(h, LANES - HEAD_DIM // 2, 1),
                                    pltpu.roll(h, HEAD_DIM // 2, 1))
                h = h * cos + partner * sin
            if kind == LOGSIG:
                z = h + bias_ref[...]
                h = -(jnp.maximum(-z, 0.0) + jnp.log1p(jnp.exp(-jnp.abs(z))))
            o32_ref[:, c * LANES:(c + 1) * LANES] = h
            o16_ref[:, c * LANES:(c + 1) * LANES] = h.astype(BF16)


def _inproj(x, g, w, gains, cos, sin, bias, kinds, tm):
    n, d = x.shape
    nout = w.shape[1]
    n_tab = cos.shape[0] // tm
    const = lambda i: (0, 0)
    return pl.pallas_call(
        functools.partial(_inproj_body, kinds=kinds),
        grid=(n // tm,),
        in_specs=[
            pl.BlockSpec((tm, d), lambda i: (i, 0)),
            pl.BlockSpec((1, d), const),
            pl.BlockSpec((d, nout), const, pipeline_mode=pl.Buffered(1)),
            pl.BlockSpec(gains.shape, const),
            pl.BlockSpec((tm, LANES), lambda i: (i % n_tab, 0)),
            pl.BlockSpec((tm, LANES), lambda i: (i % n_tab, 0)),
            pl.BlockSpec((1, LANES), const),
        ],
        out_specs=[pl.BlockSpec((tm, nout), lambda i: (i, 0)),
                   pl.BlockSpec((tm, nout), lambda i: (i, 0))],
        out_shape=[jax.ShapeDtypeStruct((n, nout), F32), jax.ShapeDtypeStruct((n, nout), BF16)],
        compiler_params=pltpu.CompilerParams(dimension_semantics=("parallel",), vmem_limit_bytes=VMEM_LIMIT),
        name="inproj",
    )(x, g.reshape(1, d), w, gains, cos, sin, bias)


def _postmix_body(*refs, n_o):
    x_ref = refs[0]
    o_refs = refs[1:1 + n_o]
    w_ref, g_ref, wup_ref, wdn_ref, out_ref = refs[1 + n_o:]
    o = o_refs[0][...] if n_o == 1 else jnp.concatenate([r[...] for r in o_refs], axis=-1)
    x1 = x_ref[...] + jnp.dot(o, w_ref[...], preferred_element_type=F32)
    xn = x1 * lax.rsqrt(jnp.mean(x1 * x1, axis=-1, keepdims=True) + RMS_EPS) * g_ref[...]
    xb = xn.astype(BF16)
    acc = x1
    d_ff = wup_ref.shape[1]
    for c in range(d_ff // FF_CHUNK):
        h = jnp.dot(xb, wup_ref[:, c * FF_CHUNK:(c + 1) * FF_CHUNK], preferred_element_type=F32)
        a = jnp.square(jnp.maximum(h, 0.0)).astype(BF16)
        acc = acc + jnp.dot(a, wdn_ref[c * FF_CHUNK:(c + 1) * FF_CHUNK, :], preferred_element_type=F32)
    out_ref[...] = acc


def _postmix(x, os_, ws, g, w_up, w_down, tm):
    n, d = x.shape
    const = lambda i: (0, 0)
    row = lambda i: (i, 0)
    w_out = ws[0] if len(ws) == 1 else jnp.concatenate(ws, axis=0)
    in_specs = [pl.BlockSpec((tm, d), row)]
    in_specs += [pl.BlockSpec((tm, o.shape[1]), row) for o in os_]
    in_specs += [pl.BlockSpec(w_out.shape, const, pipeline_mode=pl.Buffered(1)),
                 pl.BlockSpec((1, d), const),
                 pl.BlockSpec(w_up.shape, const, pipeline_mode=pl.Buffered(1)),
                 pl.BlockSpec(w_down.shape, const, pipeline_mode=pl.Buffered(1))]
    return pl.pallas_call(
        functools.partial(_postmix_body, n_o=len(os_)),
        grid=(n // tm,),
        in_specs=in_specs,
        out_specs=pl.BlockSpec((tm, d), row),
        out_shape=jax.ShapeDtypeStruct((n, d), F32),
        compiler_params=pltpu.CompilerParams(dimension_semantics=("parallel",), vmem_limit_bytes=VMEM_LIMIT),
        name="postmix",
    )(x, *os_, w_out, g.reshape(1, d), w_up, w_down)


def _gather_t_body(pt_ref, *refs, n, pps, n_steps, dup, has_new, key_major):
    del pt_ref
    page_refs = refs[:n * pps]
    new_refs = refs[n * pps:n * pps + (n if has_new else 0)]
    out_refs = refs[n * pps + len(new_refs):]

    def widen(val, a):
        return jnp.concatenate([val, val], axis=0) if dup[a] else val

    def copy_pages():
        for a in range(n):
            for u in range(pps):
                page = widen(page_refs[a * pps + u][...], a)
                where = slice(u * PAGE_SIZE, (u + 1) * PAGE_SIZE)
                if key_major:
                    out_refs[a][where, :] = page.T.astype(out_refs[a].dtype)
                else:
                    out_refs[a][:, where] = page.astype(out_refs[a].dtype)

    if not has_new:
        copy_pages()
        return
    p = pl.program_id(1)
    pl.when(p < n_steps)(copy_pages)

    @pl.when(p == n_steps)
    def _():
        for a in range(n):
            new = new_refs[a][...].astype(out_refs[a].dtype)
            if key_major:
                r = new.shape[0]
                out_refs[a][0:r, :] = new
                out_refs[a][r:, :] = jnp.zeros((pps * PAGE_SIZE - r, new.shape[1]), out_refs[a].dtype)
            else:
                new = widen(new, a)
                r = new.shape[1]
                out_refs[a][:, 0:r] = new
                out_refs[a][:, r:] = jnp.zeros((new.shape[0], pps * PAGE_SIZE - r), out_refs[a].dtype)


def _gather_t(page_table, caches, layer, news, dup, out_dtypes, pps, key_major=False):
    bs, n_pages = page_table.shape
    assert n_pages % pps == 0
    n_steps = n_pages // pps
    n = len(caches)
    has_new = news is not None
    in_specs, out_specs, out_shape = [], [], []
    for c in caches:
        for u in range(pps):
            in_specs.append(pl.BlockSpec(
                (None, None, c.shape[2], PAGE_SIZE),
                lambda b, p, pt, u=u: (layer, pt[b, jnp.minimum(p, n_steps - 1) * pps + u], 0, 0)))
    if has_new:
        for nw in news:
            in_specs.append(pl.BlockSpec((None,) + nw.shape[1:], lambda b, p, pt: (b, 0, 0)))
    s_out = (n_steps + has_new) * pps * PAGE_SIZE
    for c, d_, dt in zip(caches, dup, out_dtypes):
        feat = c.shape[2] * (2 if d_ else 1)
        if key_major:
            out_specs.append(pl.BlockSpec((None, pps * PAGE_SIZE, feat), lambda b, p, pt: (b, p, 0)))
            out_shape.append(jax.ShapeDtypeStruct((bs, s_out, feat), dt))
        else:
            out_specs.append(pl.BlockSpec((None, feat, pps * PAGE_SIZE), lambda b, p, pt: (b, 0, p)))
            out_shape.append(jax.ShapeDtypeStruct((bs, feat, s_out), dt))
    args = [a for c in caches for a in [c] * pps] + (list(news) if has_new else [])
    return pl.pallas_call(
        functools.partial(_gather_t_body, n=n, pps=pps, n_steps=n_steps, dup=tuple(dup), has_new=has_new,
                          key_major=key_major),
        grid_spec=pltpu.PrefetchScalarGridSpec(
            num_scalar_prefetch=1, grid=(bs, n_steps + has_new), in_specs=in_specs, out_specs=out_specs),
        out_shape=out_shape,
        compiler_params=pltpu.CompilerParams(dimension_semantics=("parallel", "arbitrary"),
                                             vmem_limit_bytes=VMEM_LIMIT),
        name="paged_gather_t",
    )(page_table, *args)


def _float_key(s):
    b = lax.bitcast_convert_type(s + 0.0, jnp.int32)
    return jnp.where(b >= 0, b, b ^ jnp.int32(0x7FFFFFFF))


def _prefix_tri():
    return jnp.where(_iota((LANES, LANES), 0) <= _iota((LANES, LANES), 1), 1.0, 0.0).astype(BF16)


def _topk_mask(key_ref, sel_ref, k, tri, n_used, static, additive, work=None):
    _, rows, width = key_ref.shape
    pieces = range(width // LANES)
    on, off = (0.0, NEG_BIG) if additive else (1.0, 0.0)
    kf = float(k)

    if static:
        def count(pred):
            def body(c, acc):
                for u in pieces:
                    acc = acc + jnp.where(pred(key_ref[c, :, u * LANES:(u + 1) * LANES]), 1.0, 0.0)
                return acc
            return jnp.sum(_loop(0, n_used, body, jnp.zeros((rows, LANES), F32), True), axis=-1, keepdims=True)

        zero = jnp.zeros((rows, 1), jnp.int32)
        t = jnp.where(count(lambda kc: kc >= zero) >= kf, zero, jnp.full((rows, 1), INT_MIN, jnp.int32))

        def bit_step(i, t):
            cand = t | jnp.left_shift(jnp.int32(1), 30 - i)
            return jnp.where(count(lambda kc: kc >= cand) >= kf, cand, t)

        t = lax.fori_loop(0, 31, bit_step, t)
        need = kf - count(lambda kc: kc > t)
    else:
        thr_ref, cnt_ref = work

        def count(strict):
            cnt_ref[...] = jnp.zeros((rows, LANES), F32)

            def body(c, _):
                cand = thr_ref[1]
                part = jnp.zeros((rows, LANES), F32)
                for u in pieces:
                    kc = key_ref[c, :, u * LANES:(u + 1) * LANES]
                    part = part + jnp.where((kc > cand) if strict else (kc >= cand), 1.0, 0.0)
                cnt_ref[...] += part
                return 0

            lax.fori_loop(0, n_used, body, 0)
            return jnp.sum(cnt_ref[...], axis=-1, keepdims=True)

        thr_ref[1] = jnp.zeros((rows, LANES), jnp.int32)
        thr_ref[0] = jnp.where(count(False) >= kf, thr_ref[1], jnp.full((rows, LANES), INT_MIN, jnp.int32))
        for i in range(31):
            thr_ref[1] = thr_ref[0] | jnp.int32(1 << (30 - i))
            thr_ref[0] = jnp.where(count(False) >= kf, thr_ref[1], thr_ref[0])
        thr_ref[1] = thr_ref[0]
        need = kf - count(True)
        t = thr_ref[0][:, 0:1]

    def mark(c, carry):
        for u in pieces:
            kc = key_ref[c, :, u * LANES:(u + 1) * LANES]
            eq = kc == t
            pc = jnp.dot(jnp.where(eq, 1.0, 0.0).astype(BF16), tri, preferred_element_type=F32) + carry
            sel = ((kc > t) | (eq & (pc <= need))) & (kc > KEY_NINF)
            sel_ref[c, :, u * LANES:(u + 1) * LANES] = jnp.where(sel, on, off)
            carry = jnp.broadcast_to(pc[:, LANES - 1:LANES], (rows, LANES))
        return carry

    _loop(0, n_used, mark, jnp.zeros((rows, LANES), F32), static)


def _key_chunk(ref, c, ch, kv_t):
    keys = pl.ds(_chunk_start(c, ch), ch)
    return ref[0, :, keys] if kv_t else ref[0, keys, :]


def _qk(q, kc, kv_t):
    return jnp.dot(q, kc, preferred_element_type=F32) if kv_t else _dot_nt(q, kc)


def _attend_pair(q2, k_ref, v_ref, s_ref, bounds, ch, mask_fn, static, unroll=1, kidx=None, kv_t=False):
    c_lo, c_mid, c_hi = bounds
    tq = q2.shape[0]
    pieces = range(ch // LANES)
    lo = _iota((1, LANES), 1) < HEAD_DIM
    zero = jnp.zeros_like(q2)
    qs = (jnp.where(lo, q2, zero), jnp.where(lo, zero, q2))
    chunk_of = lambda ref, c: _key_chunk(ref, c if kidx is None else kidx(c), ch, kv_t)

    def score(c, mvecs, diag):
        kc = chunk_of(k_ref, c)
        out = []
        for half in range(2):
            s = _qk(qs[half], kc, kv_t)
            extra = mask_fn(c, half, diag)
            if extra is not None:
                s = s + extra
            s_ref[half, c] = s
            mvec = mvecs[half]
            for u in pieces:
                mvec = jnp.maximum(mvec, s[:, u * LANES:(u + 1) * LANES])
            out.append(mvec)
        return tuple(out)

    mvecs = (jnp.full((tq, LANES), NEG_BIG, F32),) * 2
    mvecs = _loop(c_lo, c_mid, functools.partial(score, diag=False), mvecs, static, unroll)
    mvecs = _loop(c_mid, c_hi, functools.partial(score, diag=True), mvecs, static, unroll)
    ms = [jnp.max(mv, axis=-1, keepdims=True) for mv in mvecs]

    def accumulate(c, carry):
        vc = chunk_of(v_ref, c)
        out = []
        for half in range(2):
            lvec, acc = carry[2 * half:2 * half + 2]
            p = jnp.exp(s_ref[half, c] - ms[half])
            for u in pieces:
                lvec = lvec + p[:, u * LANES:(u + 1) * LANES]
            pv = _dot_nt(p.astype(BF16), vc) if kv_t else jnp.dot(p.astype(BF16), vc, preferred_element_type=F32)
            out += [lvec, acc + pv]
        return tuple(out)

    res = _loop(c_lo, c_hi, accumulate, (jnp.zeros((tq, LANES), F32),) * 4, static, unroll)
    outs = []
    for half in range(2):
        l = jnp.sum(res[2 * half], axis=-1, keepdims=True)
        outs.append(jnp.where(ms[half] > 0.5 * NEG_BIG, res[2 * half + 1] / jnp.maximum(l, 1e-30), 0.0))
    return jnp.where(lo, outs[0], outs[1])


def _attend_stacked(q2s, k_ref, v_ref, s_ref, n_chunks, ch, mask_fn, kv_t, kidx=None):
    tq = q2s[0].shape[0]
    n = len(q2s)
    rows = 2 * n * tq
    pieces = range(ch // LANES)
    lo = _iota((1, LANES), 1) < HEAD_DIM
    blocks = []
    for q2 in q2s:
        zero = jnp.zeros_like(q2)
        blocks += [jnp.where(lo, q2, zero), jnp.where(lo, zero, q2)]
    qst = jnp.concatenate(blocks, axis=0)
    chunk_of = lambda ref, c: _key_chunk(ref, c if kidx is None else kidx(c), ch, kv_t)

    mvec = jnp.full((rows, LANES), NEG_BIG, F32)
    for c in range(n_chunks):
        s = _qk(qst, chunk_of(k_ref, c), kv_t)
        s = s + jnp.concatenate([mask_fn(c, 0, True), mask_fn(c, 1, True)] * n, axis=0)
        s_ref[c] = s
        for u in pieces:
            mvec = jnp.maximum(mvec, s[:, u * LANES:(u + 1) * LANES])
    m = jnp.max(mvec, axis=-1, keepdims=True)
    lvec = jnp.zeros((rows, LANES), F32)
    acc = jnp.zeros((rows, LANES), F32)
    for c in range(n_chunks):
        p = jnp.exp(s_ref[c] - m)
        for u in pieces:
            lvec = lvec + p[:, u * LANES:(u + 1) * LANES]
        vc = chunk_of(v_ref, c)
        acc = acc + (_dot_nt(p.astype(BF16), vc) if kv_t else jnp.dot(p.astype(BF16), vc, preferred_element_type=F32))
    l = jnp.sum(lvec, axis=-1, keepdims=True)
    o = jnp.where(m > 0.5 * NEG_BIG, acc / jnp.maximum(l, 1e-30), 0.0)
    return [jnp.where(lo, o[2 * j * tq:(2 * j + 1) * tq], o[(2 * j + 1) * tq:(2 * j + 2) * tq]) for j in range(n)]


def _scaled(q, scale):
    return (q.astype(F32) * scale).astype(BF16)


def _query_block(qpos0, tq, axis, static):
    q0 = qpos0 if static else qpos0 + pl.program_id(axis) * tq
    return q0, q0 + _iota((tq, 1), 0)


def _causal_chunks(q0, tq, n_ch, ch):
    c_hi = _min(n_ch, _div_pow2(q0 + tq - 1, ch) + 1)
    c_mid = _min(c_hi, _div_pow2(q0 + 1, ch))
    if isinstance(c_hi, int):
        return c_mid, c_hi
    return _div_pow2(c_mid, ATT_UNROLL) * ATT_UNROLL, c_hi


def _dsa_body(qa_ref, iq_ref, misc_ref, k_ref, v_ref, ik_ref, o_ref, key_ref, sel_ref, s_ref, thr_ref, cnt_ref,
              *, tq, ch, topk, qpos0, n_keys, static, kv_t):
    n_ch = key_ref.shape[0]
    q0, qpos = _query_block(qpos0, tq, 1, static)
    _, c_hi = _causal_chunks(q0, tq, n_ch, ch)
    lo = _iota((1, LANES), 1) < HEAD_DIM
    iw = misc_ref[0][:, 0:IDX_HEADS]
    tri = _prefix_tri()

    iq_heads = []
    for hp in range(IDX_HEADS // 2):
        iqc = iq_ref[0, :, hp * LANES:(hp + 1) * LANES]
        zero = jnp.zeros_like(iqc)
        iq_heads += [jnp.where(lo, iqc, zero), jnp.where(lo, zero, iqc)]
    if static:
        iq_heads = [jnp.concatenate(iq_heads, axis=0)]

    def score_body(c, _):
        ikc = _key_chunk(ik_ref, c, ch, kv_t)
        acc = jnp.zeros((tq, ch), F32)
        for i, qm in enumerate(iq_heads):
            s_all = _qk(qm, ikc, kv_t) * (IDX_DIM ** -0.5)
            for r in range(qm.shape[0] // tq):
                h = i + r
                acc = acc + jnp.maximum(s_all[r * tq:(r + 1) * tq], 0.0) * iw[:, h:h + 1]
        sc = acc * (IDX_HEADS ** -0.5)
        kpos = c * ch + _iota((1, ch), 1)
        key_ref[c] = jnp.where((kpos <= qpos) & (kpos < n_keys), _float_key(sc), KEY_NINF)
        return 0

    _loop(0, c_hi, score_body, 0, static, ATT_UNROLL)
    _topk_mask(key_ref, sel_ref, topk, tri, c_hi, static, additive=True, work=(thr_ref, cnt_ref))

    n_pairs = qa_ref.shape[2] // LANES
    q2s = [_scaled(qa_ref[0, :, j * LANES:(j + 1) * LANES], HEAD_DIM ** -0.5) for j in range(n_pairs)]
    sel_mask = lambda c, half, diag: sel_ref[c]
    if static:
        outs = _attend_stacked(q2s, k_ref, v_ref, s_ref, c_hi, ch, sel_mask, kv_t)
    for j in range(n_pairs):
        o = outs[j] if static else _attend_pair(q2s[j], k_ref, v_ref, s_ref, (0, 0, c_hi), ch, sel_mask,
                                                static, ATT_UNROLL, kv_t=kv_t)
        o_ref[0, :, j * LANES:(j + 1) * LANES] = o.astype(o_ref.dtype)


def _stage_shape(n_slots, tq, ch, static, n_pairs=4):
    return (n_slots, 2 * n_pairs * tq, ch) if static else (2, n_slots, tq, ch)


def _kv_len(arr, kv_t):
    return arr.shape[2] if kv_t else arr.shape[1]


def _kv_spec(operand, kv_t, chunk_of_step=lambda *g: 0):
    arr, c0 = operand
    s = _kv_len(arr, kv_t)
    if kv_t:
        return pl.BlockSpec((1, LANES, s), lambda bi, *g: (bi, c0 + chunk_of_step(*g), 0))
    return pl.BlockSpec((1, s, LANES), lambda bi, *g: (bi, 0, c0 + chunk_of_step(*g)))


def _dsa(q16, m32, src, col, tq, topk, qpos0, n_keys, static, kv_t):
    b, t, _ = q16.shape
    s = _kv_len(src["k"][0], kv_t)
    ch = KEY_CHUNK
    n_qa = 4 * LANES
    kv_spec = lambda name: _kv_spec(src[name], kv_t)
    return pl.pallas_call(
        functools.partial(_dsa_body, tq=tq, ch=ch, topk=topk, qpos0=qpos0, n_keys=n_keys, static=static,
                          kv_t=kv_t),
        grid=(b, t // tq),
        in_specs=[
            pl.BlockSpec((1, tq, n_qa), lambda bi, i: (bi, i, col["qa"] * LANES // n_qa)),
            pl.BlockSpec((1, tq, 2 * LANES), lambda bi, i: (bi, i, col["iq"] // 2)),
            pl.BlockSpec((1, tq, LANES), lambda bi, i: (bi, i, col["misc"])),
            kv_spec("k"), kv_spec("v"), kv_spec("ik"),
        ],
        out_specs=pl.BlockSpec((1, tq, n_qa), lambda bi, i: (bi, i, 0)),
        out_shape=jax.ShapeDtypeStruct((b, t, n_qa), BF16),
        scratch_shapes=[pltpu.VMEM((s // ch, tq, ch), jnp.int32), pltpu.VMEM((s // ch, tq, ch), F32),
                        pltpu.VMEM(_stage_shape(s // ch, tq, ch, static), F32),
                        pltpu.VMEM((2, tq, LANES), jnp.int32), pltpu.VMEM((tq, LANES), F32)],
        compiler_params=pltpu.CompilerParams(dimension_semantics=("parallel", "arbitrary"),
                                             vmem_limit_bytes=VMEM_LIMIT),
        name="dsa",
    )(q16, q16, m32, src["k"][0], src["v"][0], src["ik"][0])


def _nsa_body(qb_ref, misc_ref, cmpk_ref, cmpv_ref, sk_ref, sv_ref, wk_ref, wv_ref, imp_ref, exp_ref,
              o_ref, key_ref, selb_ref, msk_ref, s_ref, sw_ref,
              *, tq, ch, n_sel, qpos0, wpos0, n_wvalid, static, kv_t):
    n_ch = exp_ref.shape[0]
    ncp = cmpk_ref.shape[1]
    nsp = imp_ref.shape[1]
    q0, qpos = _query_block(qpos0, tq, 1, static)
    _, c_hi = _causal_chunks(q0, tq, n_ch, ch)
    lo = _iota((1, LANES), 1) < HEAD_DIM
    scale = HEAD_DIM ** -0.5
    tri = _prefix_tri()
    n_chunks_q = qb_ref.shape[2] // LANES
    gates = jax.nn.sigmoid(misc_ref[0])

    cvalid = (_iota((1, ncp), 1) * CMP_STRIDE + (CMP_LEN - 1)) <= qpos
    ck = cmpk_ref[0]
    cv = cmpv_ref[0]
    imp_c = [jnp.zeros((tq, ncp), F32), jnp.zeros((tq, ncp), F32)]
    o_cmp = []
    for j in range(n_chunks_q):
        q2 = qb_ref[0, :, j * LANES:(j + 1) * LANES]
        zero = jnp.zeros_like(q2)
        outs = []
        for half in range(2):
            qm = jnp.where(lo, q2, zero) if half == 0 else jnp.where(lo, zero, q2)
            s = jnp.where(cvalid, _dot_nt(qm, ck) * scale, NEG_BIG)
            p = jnp.where(cvalid, jnp.exp(s - jnp.max(s, axis=-1, keepdims=True)), 0.0)
            pn = p / jnp.maximum(jnp.sum(p, axis=-1, keepdims=True), 1e-30)
            imp_c[half] = imp_c[half] + pn
            outs.append(jnp.dot(pn.astype(BF16), cv, preferred_element_type=F32))
        o_cmp.append(jnp.where(lo, outs[0], outs[1]))

    blk = _iota((1, nsp), 1)
    adm = blk * SEL_LEN <= qpos
    cur = _div_pow2(qpos, SEL_LEN)
    forced = adm & ((blk == 0) | (blk == cur) | (blk == cur - 1))
    for g in range(2):
        imp = jnp.dot(imp_c[g], imp_ref[...], preferred_element_type=F32, precision=lax.Precision.HIGHEST)
        score = jnp.where(forced, FORCE_SCORE, jnp.where(adm, imp, -jnp.inf))
        key_ref[0, g * tq:(g + 1) * tq, :] = _float_key(score)
    _topk_mask(key_ref, selb_ref, n_sel, tri, 1, True, additive=False)
    for g in range(2):
        bm = selb_ref[0, g * tq:(g + 1) * tq, :].astype(BF16)

        def expand(c, _):
            kpos = c * ch + _iota((1, ch), 1)
            hit = (jnp.dot(bm, exp_ref[c], preferred_element_type=F32) > 0.5) & (kpos <= qpos)
            msk_ref[g, c] = jnp.where(hit, 0.0, NEG_BIG).astype(msk_ref.dtype)
            return 0

        _loop(0, c_hi, expand, 0, static, ATT_UNROLL)

    wch = WIN_CHUNK
    n_wch = wk_ref.shape[2 if kv_t else 1] // wch
    n_win = sw_ref.shape[0 if static else 1]
    wc0 = _div_pow2(q0 - WINDOW - wpos0, wch)

    def win_chunk(u):
        c = wc0 + u
        return _min(_max(c, 0), n_wch - 1)

    def win_mask(u, half, diag):
        row = (wc0 + u) * wch + _iota((1, wch), 1)
        kpos = wpos0 + row
        dlt = qpos - kpos
        ok = (dlt >= 0) & (dlt <= WINDOW) & (kpos >= 0) & (row >= 0) & (row < n_wvalid)
        return jnp.where(ok, 0.0, NEG_BIG)

    q2s = [_scaled(qb_ref[0, :, j * LANES:(j + 1) * LANES], scale) for j in range(n_chunks_q)]
    sel_mask = lambda c, half, diag: msk_ref[half, c].astype(F32)
    if static:
        o_sels = _attend_stacked(q2s, sk_ref, sv_ref, s_ref, c_hi, ch, sel_mask, kv_t)
        o_wins = _attend_stacked(q2s, wk_ref, wv_ref, sw_ref, n_win, wch, win_mask, kv_t, kidx=win_chunk)
    for j in range(n_chunks_q):
        if static:
            o_sel, o_win = o_sels[j], o_wins[j]
        else:
            o_sel = _attend_pair(q2s[j], sk_ref, sv_ref, s_ref, (0, 0, c_hi), ch, sel_mask, static, ATT_UNROLL,
                                 kv_t=kv_t)
            o_win = _attend_pair(q2s[j], wk_ref, wv_ref, sw_ref, (0, 0, n_win), wch, win_mask, True, kidx=win_chunk,
                                 kv_t=kv_t)
        o = jnp.zeros((tq, LANES), F32)
        for bi, ob in enumerate((o_cmp[j], o_sel, o_win)):
            c_a = IDX_HEADS + 3 * j + bi
            c_b = IDX_HEADS + 3 * (n_chunks_q + j) + bi
            o = o + jnp.where(lo, gates[:, c_a:c_a + 1], gates[:, c_b:c_b + 1]) * ob
        o_ref[0, :, j * LANES:(j + 1) * LANES] = o.astype(o_ref.dtype)


def _nsa(q16, m32, cmpk, cmpv, src, imp_mat, exp_mat, col, tq, n_sel, qpos0, wpos0, n_wvalid, static, kv_t):
    b, t, _ = q16.shape
    s = _kv_len(src["sk"][0], kv_t)
    ch = exp_mat.shape[2]
    ncp, nsp = imp_mat.shape
    n_qb = 4 * LANES
    assert t == tq or tq % WIN_CHUNK == 0
    mis = (qpos0 - WINDOW - wpos0) % WIN_CHUNK
    n_win = (mis + WINDOW + tq - 1) // WIN_CHUNK + 1
    spec = lambda name: _kv_spec(src[name], kv_t)
    return pl.pallas_call(
        functools.partial(_nsa_body, tq=tq, ch=ch, n_sel=n_sel, qpos0=qpos0, wpos0=wpos0, n_wvalid=n_wvalid,
                          static=static, kv_t=kv_t),
        grid=(b, t // tq),
        in_specs=[
            pl.BlockSpec((1, tq, n_qb), lambda bi, i: (bi, i, col["qb"] * LANES // n_qb)),
            pl.BlockSpec((1, tq, LANES), lambda bi, i: (bi, i, col["misc"])),
            pl.BlockSpec((1, ncp, LANES), lambda bi, i: (bi, 0, 0)),
            pl.BlockSpec((1, ncp, LANES), lambda bi, i: (bi, 0, 0)),
            spec("sk"), spec("sv"), spec("wk"), spec("wv"),
            pl.BlockSpec((ncp, nsp), lambda bi, i: (0, 0)),
            pl.BlockSpec(exp_mat.shape, lambda bi, i: (0, 0, 0)),
        ],
        out_specs=pl.BlockSpec((1, tq, n_qb), lambda bi, i: (bi, i, 0)),
        out_shape=jax.ShapeDtypeStruct((b, t, n_qb), BF16),
        scratch_shapes=[pltpu.VMEM((1, 2 * tq, nsp), jnp.int32), pltpu.VMEM((1, 2 * tq, nsp), F32),
                        pltpu.VMEM((2, s // ch, tq, ch), BF16), pltpu.VMEM(_stage_shape(s // ch, tq, ch, static), F32),
                        pltpu.VMEM(_stage_shape(n_win, tq, WIN_CHUNK, static), F32)],
        compiler_params=pltpu.CompilerParams(dimension_semantics=("parallel", "arbitrary"),
                                             vmem_limit_bytes=VMEM_LIMIT),
        name="nsa",
    )(q16, m32, cmpk, cmpv, src["sk"][0], src["sv"][0], src["wk"][0], src["wv"][0], imp_mat, exp_mat)


def _compress_body(uk_ref, uv_ref, wk_ref, wv_ref, pe_ref, ok_ref, ov_ref, shift_ref):
    nu = uk_ref.shape[1]
    half = wk_ref.shape[0] // 2
    shift_ref[nu:nu + 8, :] = jnp.zeros((8, LANES), F32)
    for u_ref, w_ref, o_ref in ((uk_ref, wk_ref, ok_ref), (uv_ref, wv_ref, ov_ref)):
        u = u_ref[0]
        top = jnp.dot(u, w_ref[0:half, :], preferred_element_type=F32)
        shift_ref[0:nu, :] = jnp.dot(u, w_ref[half:2 * half, :], preferred_element_type=F32)
        pec = jnp.dot(pe_ref[...], w_ref[...], preferred_element_type=F32)[0:1, :]
        o_ref[0] = (top + shift_ref[1:nu + 1, :] + pec).astype(o_ref.dtype)


def _compress(uk, uv, wk2, wv2, pe2):
    b, nu, width = uk.shape
    const = lambda bi: (0, 0)
    row = lambda bi: (bi, 0, 0)
    return pl.pallas_call(
        _compress_body,
        grid=(b,),
        in_specs=[pl.BlockSpec((1, nu, width), row), pl.BlockSpec((1, nu, width), row),
                  pl.BlockSpec(wk2.shape, const), pl.BlockSpec(wv2.shape, const), pl.BlockSpec(pe2.shape, const)],
        out_specs=[pl.BlockSpec((1, nu, LANES), row), pl.BlockSpec((1, nu, LANES), row)],
        out_shape=[jax.ShapeDtypeStruct((b, nu, LANES), BF16)] * 2,
        scratch_shapes=[pltpu.VMEM((nu + 8, LANES), F32)],
        compiler_params=pltpu.CompilerParams(dimension_semantics=("parallel",), vmem_limit_bytes=VMEM_LIMIT),
        name="compress",
    )(uk, uv, wk2, wv2, pe2)


def _cumsum_body(x_ref, o_ref):
    s = x_ref.shape[2]
    tri = jnp.where(_iota((LANES, LANES), 0) <= _iota((LANES, LANES), 1), 1.0, 0.0)
    carry = jnp.zeros((x_ref.shape[1], 1), F32)
    for c in range(s // LANES):
        blk = x_ref[0, :, c * LANES:(c + 1) * LANES]
        pc = jnp.dot(blk, tri, preferred_element_type=F32, precision=lax.Precision.HIGHEST) + carry
        o_ref[0, :, c * LANES:(c + 1) * LANES] = pc
        carry = pc[:, LANES - 1:LANES]


def _cumsum(x):
    b, h, s = x.shape
    return pl.pallas_call(
        _cumsum_body,
        grid=(b,),
        in_specs=[pl.BlockSpec((1, h, s), lambda bi: (bi, 0, 0))],
        out_specs=pl.BlockSpec((1, h, s), lambda bi: (bi, 0, 0)),
        out_shape=jax.ShapeDtypeStruct((b, h, s), F32),
        compiler_params=pltpu.CompilerParams(dimension_semantics=("parallel",)),
        name="cumsum",
    )(x)


def _fox_body(q_ref, k_ref, v_ref, cq_ref, ck_ref, o_ref, s_ref, *, tq, ch, qpos0, n_keys, static, kv_t):
    n_ch = ck_ref.shape[2]
    q0, qpos = _query_block(qpos0, tq, 2, static)
    c_mid, c_hi = _causal_chunks(q0, tq, n_ch, ch)
    cq = cq_ref[0, 0]

    def bias_fn(c, half, diag):
        bias = cq[:, half:half + 1] - ck_ref[0, 0, c, half:half + 1, :]
        if diag:
            kpos = c * ch + _iota((1, ch), 1)
            bias = jnp.where((kpos <= qpos) & (kpos < n_keys), bias, NEG_BIG)
        return bias

    q2 = _scaled(q_ref[0], HEAD_DIM ** -0.5)
    if static:
        (o,) = _attend_stacked([q2], k_ref, v_ref, s_ref, c_hi, ch, lambda c, half, diag: bias_fn(c, half, c >= c_mid),
                               kv_t)
    else:
        o = _attend_pair(q2, k_ref, v_ref, s_ref, (0, c_mid, c_hi), ch, bias_fn, static, ATT_UNROLL, kv_t=kv_t)
    o_ref[0] = o.astype(o_ref.dtype)


def _fox(q16, src, cq, ck, qcol, tq, qpos0, n_keys, static, kv_t):
    b, t, _ = q16.shape
    s = _kv_len(src["k"][0], kv_t)
    n_pairs = cq.shape[1]
    ch = ck.shape[4]
    kv_spec = lambda name: _kv_spec(src[name], kv_t, lambda hp, i: hp)
    return pl.pallas_call(
        functools.partial(_fox_body, tq=tq, ch=ch, qpos0=qpos0, n_keys=n_keys, static=static, kv_t=kv_t),
        grid=(b, n_pairs, t // tq),
        in_specs=[
            pl.BlockSpec((1, tq, LANES), lambda bi, hp, i: (bi, i, qcol + hp)),
            kv_spec("k"), kv_spec("v"),
            pl.BlockSpec((1, 1, tq, 2), lambda bi, hp, i: (bi, hp, i, 0)),
            pl.BlockSpec((1, 1, s // ch, 2, ch), lambda bi, hp, i: (bi, hp, 0, 0, 0)),
        ],
        out_specs=pl.BlockSpec((1, tq, LANES), lambda bi, hp, i: (bi, i, hp)),
        out_shape=jax.ShapeDtypeStruct((b, t, n_pairs * LANES), BF16),
        scratch_shapes=[pltpu.VMEM(_stage_shape(s // ch, tq, ch, static, n_pairs=1), F32)],
        compiler_params=pltpu.CompilerParams(dimension_semantics=("parallel", "parallel", "arbitrary"),
                                             vmem_limit_bytes=VMEM_LIMIT),
        name="fox",
    )(q16, src["k"][0], src["v"][0], cq, ck)


E_QA, E_KA, E_VA, E_IQ, E_QB, E_IK, E_CK, E_CV, E_SK, E_SV, E_WK, E_WV, E_MISC = 0, 4, 5, 6, 8, 12, 13, 14, 15, 16, 17, 18, 19
G_QA, G_KA, G_QB, G_CMP, G_SEL, G_WIN = range(6)
E_KINDS = ((NORM_ROPE, G_QA),) * 4 + ((NORM_ROPE, G_KA), (PLAIN, 0)) + ((ROPE, 0),) * 2 + ((NORM_ROPE, G_QB),) * 4 + (
    (ROPE, 0), (NORM_ROPE, G_CMP), (PLAIN, 0), (NORM_ROPE, G_SEL), (PLAIN, 0), (NORM_ROPE, G_WIN), (PLAIN, 0), (PLAIN, 0))
O_Q, O_K, O_V, O_MISC = 0, 8, 16, 24
O_KINDS = ((NORM, 0),) * 8 + ((NORM, 1),) * 8 + ((PLAIN, 0),) * 8 + ((LOGSIG, 0),)


def _pair_heads(w, n_groups):
    d = w.shape[0]
    nj = w.shape[1] // (n_groups * HEAD_DIM)
    return w.reshape(d, n_groups, nj, HEAD_DIM).transpose(0, 2, 1, 3).reshape(d, -1)


def _even_weight(w_in):
    d = w_in.shape[0]
    sizes = (512, 128, 128, 256, 64, 4, 512, 128, 128, 128, 128, 128, 128, 24)
    offs = np.cumsum((0,) + sizes)
    qa, ka, va, iq, ik, iw, qb, ck, cv, sk, sv, wk, wv, gl = (w_in[:, offs[i]:offs[i + 1]] for i in range(14))
    misc = jnp.concatenate([iw, gl, jnp.zeros((d, LANES - 28), w_in.dtype)], axis=1)
    cols = [_pair_heads(qa, A_KV), ka, va, iq, _pair_heads(qb, B_KV), ik, ik, ck, cv, sk, sv, wk, wv, misc]
    return jnp.concatenate(cols, axis=1).astype(BF16)


def _odd_weight(w_in):
    d = w_in.shape[0]
    n = O_MISC * LANES
    fl = w_in[:, n:]
    return jnp.concatenate([w_in[:, :n], fl, jnp.zeros((d, LANES - fl.shape[1]), w_in.dtype)], axis=1).astype(BF16)


def _gain_table(gains):
    rows = [jnp.tile(g, 2) for g in gains]
    rows += [jnp.zeros((LANES,), F32)] * (8 - len(rows))
    return jnp.stack(rows)


def _rope_tables(pos):
    half = HEAD_DIM // 2
    inv = ROPE_THETA ** (-jnp.arange(half, dtype=F32) / half)
    ang = pos.astype(F32)[:, None] * inv[None, :]
    cos, sin = jnp.cos(ang), jnp.sin(ang)
    return jnp.tile(jnp.concatenate([cos, cos], axis=1), (1, 2)), jnp.tile(jnp.concatenate([-sin, sin], axis=1), (1, 2))


def _pair_rows(w_out_half, n_groups):
    nj = w_out_half.shape[0] // (n_groups * HEAD_DIM)
    return w_out_half.reshape(n_groups, nj, HEAD_DIM, -1).transpose(1, 0, 2, 3).reshape(w_out_half.shape)


def _compress_weight(w):
    w3 = w.reshape(CMP_LEN, HEAD_DIM, HEAD_DIM)
    eye = jnp.eye(B_KV, dtype=w.dtype)
    w5 = w3[:, None, :, None, :] * eye[None, :, None, :, None]
    return w5.reshape(CMP_LEN * B_KV * HEAD_DIM, B_KV * HEAD_DIM).astype(BF16)


def _importance_matrix(nc, ncp, ns, nsp):
    r, m = SEL_LEN // CMP_STRIDE, CMP_LEN // CMP_STRIDE
    mat = np.zeros((ncp, nsp), np.float32)
    for n in range(ns):
        for a in range(r):
            for b_ in range(m):
                c = n * r - m + 1 + a + b_
                if 0 <= c < nc:
                    mat[c, n] += 1.0
    return jnp.asarray(mat)


def _expand_matrix(nsp, s, ch):
    blk = np.arange(s) // SEL_LEN
    mat = (np.arange(nsp)[:, None] == blk[None, :]).astype(np.float32)
    return jnp.asarray(mat.reshape(nsp, s // ch, ch).transpose(1, 0, 2), dtype=BF16)


def _chunk(a, c, n=1):
    return a[..., c * LANES:(c + n) * LANES]


def _pad_rows(a, rows):
    return jnp.pad(a, ((0, 0), (0, rows - a.shape[1]), (0, 0)))


def kernel(x_prompt, x_sample, cache_a_k, cache_a_v, cache_a_idx, cache_b_cmp_k, cache_b_cmp_v, cache_b_sel_k, cache_b_sel_v, state_b_win_k, state_b_win_v, cache_c_k, cache_c_v, cache_c_logf, page_table, norm_mix, norm_ffn, e_w_in, e_qn_a, e_kn_a, e_qn_b, e_kn_cmp, e_kn_sel, e_kn_win, e_cmp_pos, e_w_cmp_k, e_w_cmp_v, e_w_out, o_w_in, o_b_f, o_qn, o_kn, o_w_out, w_up, w_down):
    bp, t, d = x_prompt.shape
    bs, ts, _ = x_sample.shape
    depth = norm_mix.shape[0]
    n_pages = page_table.shape[1]
    past = n_pages * PAGE_SIZE
    n_keys_s = past + ts
    s_s = past + KEY_CHUNK
    assert ts <= TQ_SAMPLE and t % KEY_CHUNK == 0 and past % KEY_CHUNK == 0
    n_buf = state_b_win_k.shape[2]
    c_heads = cache_c_k.shape[3]

    xp = x_prompt.reshape(bp * t, d)
    xs = x_sample.reshape(bs * ts, d)
    cos_p, sin_p = _rope_tables(jnp.arange(t))
    cos_s, sin_s = (jnp.tile(a, (bs, 1)) for a in _rope_tables(past + jnp.arange(ts)))
    zero_bias = jnp.zeros((1, LANES), F32)

    def feat_major(c):
        if c.ndim == 5:
            return c.transpose(0, 1, 3, 4, 2).reshape(c.shape[:2] + (c.shape[3] * c.shape[4], c.shape[2]))
        return c.transpose(0, 1, 3, 2)

    def key_major(a, heads):
        return a.reshape(a.shape[0], heads, a.shape[1] // heads, a.shape[2]).transpose(0, 3, 1, 2)

    def nsa_consts(n_keys, s):
        nc = (n_keys - CMP_LEN) // CMP_STRIDE + 1
        ncp = _round_up(nc, LANES)
        ns = -(-n_keys // SEL_LEN)
        nsp = _round_up(ns, LANES)
        return ncp, min(SEL_TOP, ns), _importance_matrix(nc, ncp, ns, nsp), _expand_matrix(nsp, s, KEY_CHUNK)

    def fit_rows(a, rows):
        return a[:, :rows] if a.shape[1] >= rows else _pad_rows(a, rows)

    ev_p, ev_s, od_p, od_s = [], [], [], []
    for li in range(depth):
        j = li // 2
        if li % 2 == 0:
            w = _even_weight(e_w_in[j])
            gains = _gain_table([e_qn_a[j], e_kn_a[j], e_qn_b[j], e_kn_cmp[j], e_kn_sel[j], e_kn_win[j]])
            hp32, hp16 = _inproj(xp, norm_mix[li], w, gains, cos_p, sin_p, zero_bias, E_KINDS, TM_ROWS)
            hs32, hs16 = _inproj(xs, norm_mix[li], w, gains, cos_s, sin_s, zero_bias, E_KINDS, bs * ts)
            hp32, hp16 = hp32.reshape(bp, t, -1), hp16.reshape(bp, t, -1)
            hs32, hs16 = hs32.reshape(bs, ts, -1), hs16.reshape(bs, ts, -1)
            wk2, wv2 = _compress_weight(e_w_cmp_k[j]), _compress_weight(e_w_cmp_v[j])
            pe2 = jnp.pad(jnp.tile(e_cmp_pos[j][:, None, :], (1, B_KV, 1)).reshape(1, -1),
                          ((0, BF16_SUBLANES - 1), (0, 0))).astype(BF16)
            cu = CMP_STRIDE * LANES
            col = dict(qa=E_QA, iq=E_IQ, misc=E_MISC, qb=E_QB)

            st_p = [_chunk(hp32, c) for c in (E_KA, E_VA)]
            st_p.append(_chunk(hp32, E_IK)[..., :IDX_DIM])
            st_p += [_chunk(hp32, c) for c in (E_CK, E_CV, E_SK, E_SV)]
            n_keep = min(WINDOW, t)
            st_p += [_chunk(hp32, c)[:, t - n_keep:] for c in (E_WK, E_WV)]
            ncp, n_sel, imp_mat, exp_mat = nsa_consts(t, t)
            cmpk, cmpv = _compress(_chunk(hp16, E_CK).reshape(bp, t // CMP_STRIDE, cu),
                                   _chunk(hp16, E_CV).reshape(bp, t // CMP_STRIDE, cu), wk2, wv2, pe2)
            src = {name: (hp16, c) for name, c in (("k", E_KA), ("v", E_VA), ("ik", E_IK), ("sk", E_SK),
                                                    ("sv", E_SV), ("wk", E_WK), ("wv", E_WV))}
            oa_p = _dsa(hp16, hp32, src, col, TQ_PROMPT, min(DSA_TOPK, t // 4), 0, t, False, False)
            ob_p = _nsa(hp16, hp32, fit_rows(cmpk, ncp), fit_rows(cmpv, ncp), src, imp_mat, exp_mat,
                        col, TQ_PROMPT_NSA, n_sel, 0, 0, t, False, False)

            st_s = [_chunk(hs32, c) for c in (E_KA, E_VA)]
            st_s.append(_chunk(hs32, E_IK)[..., :IDX_DIM])
            st_s += [_chunk(hs32, c) for c in (E_CK, E_CV, E_SK, E_SV)]
            wk_all = jnp.concatenate([feat_major(state_b_win_k)[j], jnp.swapaxes(_chunk(hs32, E_WK), 1, 2)], axis=2)
            wv_all = jnp.concatenate([feat_major(state_b_win_v)[j], jnp.swapaxes(_chunk(hs32, E_WV), 1, 2)], axis=2)
            n_keep = min(WINDOW, n_buf + ts)
            st_s += [key_major(a[:, :, n_buf + ts - n_keep:], B_KV) for a in (wk_all, wv_all)]
            caches = [feat_major(c) for c in (cache_a_k, cache_a_v, cache_a_idx, cache_b_sel_k, cache_b_sel_v)]
            news = [jnp.swapaxes(_pad_rows(st_s[n], TQ_SAMPLE), 1, 2) for n in (0, 1, 2, 5, 6)]
            gk, gv, gik, gsk, gsv = _gather_t(
                page_table, caches, j, news, (False, False, True, False, False), (BF16,) * 5, GATHER_PAGES)
            gck, gcv = _gather_t(page_table, [feat_major(cache_b_cmp_k), feat_major(cache_b_cmp_v)], j,
                                 [_pad_rows(st_s[3], TQ_SAMPLE), _pad_rows(st_s[4], TQ_SAMPLE)],
                                 (False, False), (BF16, BF16), GATHER_PAGES, key_major=True)
            qs16, qs32 = _pad_rows(hs16, TQ_SAMPLE), _pad_rows(hs32, TQ_SAMPLE)
            ncp, n_sel, imp_mat, exp_mat = nsa_consts(n_keys_s, s_s)
            cmpk, cmpv = _compress(gck.reshape(bs, s_s // CMP_STRIDE, cu), gcv.reshape(bs, s_s // CMP_STRIDE, cu),
                                   wk2, wv2, pe2)
            sw = _round_up(n_buf + ts, WIN_CHUNK)
            pad_keys = lambda a: jnp.pad(a, ((0, 0), (0, 0), (0, sw - a.shape[2]))).astype(BF16)
            src = dict(k=(gk, 0), v=(gv, 0), ik=(gik, 0), sk=(gsk, 0), sv=(gsv, 0),
                       wk=(pad_keys(wk_all), 0), wv=(pad_keys(wv_all), 0))
            oa_s = _dsa(qs16, qs32, src, col, TQ_SAMPLE, min(DSA_TOPK, n_keys_s // 4), past, n_keys_s, True, True)
            ob_s = _nsa(qs16, qs32, fit_rows(cmpk, ncp), fit_rows(cmpv, ncp), src, imp_mat, exp_mat,
                        col, TQ_SAMPLE, n_sel, past, past - n_buf, n_buf + ts, True, True)
            ev_p.append(st_p)
            ev_s.append(st_s)
            half = e_w_out.shape[1] // 2
            w_outs = [_pair_rows(e_w_out[j][:half], A_KV).astype(BF16), _pair_rows(e_w_out[j][half:], B_KV).astype(BF16)]
            os_p = [oa_p.reshape(bp * t, -1), ob_p.reshape(bp * t, -1)]
            os_s = [oa_s[:, :ts].reshape(bs * ts, -1), ob_s[:, :ts].reshape(bs * ts, -1)]
        else:
            w = _odd_weight(o_w_in[j])
            gains = _gain_table([o_qn[j], o_kn[j]])
            bias = jnp.pad(o_b_f[j], (0, LANES - c_heads)).reshape(1, LANES)
            hp32, hp16 = _inproj(xp, norm_mix[li], w, gains, cos_p, sin_p, bias, O_KINDS, TM_ROWS)
            hs32, hs16 = _inproj(xs, norm_mix[li], w, gains, cos_s, sin_s, bias, O_KINDS, bs * ts)
            hp32, hp16 = hp32.reshape(bp, t, -1), hp16.reshape(bp, t, -1)
            hs32, hs16 = hs32.reshape(bs, ts, -1), hs16.reshape(bs, ts, -1)
            n_pairs = c_heads // 2

            def fox_bias(cum, qlo, tq_pad):
                b_, _, s_ = cum.shape
                cq = cum[:, :, qlo:qlo + tq_pad].reshape(b_, n_pairs, 2, tq_pad).transpose(0, 1, 3, 2)
                ck = cum.reshape(b_, n_pairs, 2, s_ // KEY_CHUNK, KEY_CHUNK).transpose(0, 1, 3, 2, 4)
                return cq, ck

            logf_p = _chunk(hp32, O_MISC)[..., :c_heads]
            st_p = [_chunk(hp32, O_K, 8), _chunk(hp32, O_V, 8), logf_p]
            cum_p = _cumsum(logf_p.transpose(0, 2, 1))
            cq, ck = fox_bias(cum_p, 0, t)
            o_p = _fox(hp16, dict(k=(hp16, O_K), v=(hp16, O_V)), cq, ck, O_Q, TQ_PROMPT, 0, t, False, False)

            logf_s = _chunk(hs32, O_MISC)[..., :c_heads]
            st_s = [_chunk(hs32, O_K, 8), _chunk(hs32, O_V, 8), logf_s]
            news = [jnp.swapaxes(_pad_rows(a, TQ_SAMPLE), 1, 2) for a in st_s]
            gk, gv = _gather_t(page_table, [feat_major(cache_c_k), feat_major(cache_c_v)], j, news[:2],
                               (False, False), (BF16, BF16), GATHER_PAGES)
            (glogf,) = _gather_t(page_table, [feat_major(cache_c_logf)], j, None, (False,), (F32,), min(LOGF_PAGES, n_pages))
            cum_s = _cumsum(jnp.pad(jnp.concatenate([glogf, news[2]], axis=2),
                                    ((0, 0), (0, 0), (0, KEY_CHUNK - TQ_SAMPLE))))
            cq, ck = fox_bias(cum_s, past, TQ_SAMPLE)
            o_s = _fox(_pad_rows(hs16, TQ_SAMPLE), dict(k=(gk, 0), v=(gv, 0)), cq, ck, O_Q, TQ_SAMPLE, past, n_keys_s,
                       True, True)
            od_p.append(st_p)
            od_s.append(st_s)
            w_outs = [o_w_out[j].astype(BF16)]
            os_p = [o_p.reshape(bp * t, -1)]
            os_s = [o_s[:, :ts].reshape(bs * ts, -1)]
        wu, wd = w_up[li].astype(BF16), w_down[li].astype(BF16)
        xp = _postmix(xp, os_p, w_outs, norm_ffn[li], wu, wd, TM_ROWS)
        xs = _postmix(xs, os_s, w_outs, norm_ffn[li], wu, wd, bs * ts)

    outs = [xp.reshape(bp, t, d), xs.reshape(bs, ts, d)]
    n_even = len(ev_p)
    for n in range(9):
        for states, b_ in ((ev_p, bp), (ev_s, bs)):
            rows = states[0][n].shape[1]
            tail = (IDX_DIM,) if n == 2 else (A_KV, HEAD_DIM)
            outs.append(jnp.stack([s[n] for s in states]).reshape((n_even, b_, rows) + tail))
    for n in range(3):
        for states, b_, t_ in ((od_p, bp, t), (od_s, bs, ts)):
            tail = (c_heads,) if n == 2 else (c_heads, HEAD_DIM)
            outs.append(jnp.stack([s[n] for s in states]).reshape((len(states), b_, t_) + tail))
    return tuple(outs)
```
